```python
import jax, jax.numpy as jnp
from jax import lax
import numpy as np

D_MODEL = 1024
BATCH = 32
SEQ = 2048
DEPTH = 4
DEC_BATCH = 4
DEC_SEQ = 8192
PAST_LEN = 128

N_MIXERS = 2
N_GLA = (DEPTH + 1) // 2
N_SG = DEPTH // 2
GLA_HEADS = 4
GLA_DK = D_MODEL // 2
GLA_DV = D_MODEL
GLA_HK = GLA_DK // GLA_HEADS
GLA_HV = GLA_DV // GLA_HEADS
GLA_RANK = 16
GLA_TAU = 16.0
GLA_CHUNK = 64
SG_CHUNK = 128
SG_GROUPS = 4
SG_WIDTH = D_MODEL
SG_GD = SG_WIDTH // SG_GROUPS
FFN_HIDDEN = -(-8 * D_MODEL // 768) * 256
EPS = 1e-6

kernel_name = 'hybrid_gla_sgu_adaln_encoder'


def rms_norm(x, g):
    xf = x.astype(jnp.float32)
    y = xf * lax.rsqrt(jnp.mean(xf * xf, axis=-1, keepdims=True) + EPS)
    return (y * g.astype(jnp.float32)).astype(x.dtype)


def layer_norm(x, g, b):
    xf = x.astype(jnp.float32)
    mu = jnp.mean(xf, axis=-1, keepdims=True)
    xc = xf - mu
    y = xc * lax.rsqrt(jnp.mean(xc * xc, axis=-1, keepdims=True) + EPS)
    return (y * g.astype(jnp.float32) + b.astype(jnp.float32)).astype(x.dtype)


def gla_direction(q, k, v, log_a, include_diag):
    B, H, L, dk = q.shape
    dv = v.shape[-1]
    C = GLA_CHUNK
    n = L // C
    q = q.reshape(B, H, n, C, dk)
    k = k.reshape(B, H, n, C, dk)
    v = v.reshape(B, H, n, C, dv)
    b = jnp.cumsum(log_a.reshape(B, H, n, C, dk), axis=3)
    b_last = b[:, :, :, -1:, :]
    q_d = q * jnp.exp(b)
    k_d = k * jnp.exp(-b)
    k_end = k * jnp.exp(b_last - b)
    scores = jnp.einsum('bhncd,bhnsd->bhncs', q_d, k_d)
    mask = jnp.tril(jnp.ones((C, C), dtype=bool), 0 if include_diag else -1)
    scores = jnp.where(mask, scores, 0.0)
    o_intra = jnp.einsum('bhncs,bhnsv->bhncv', scores, v)
    decay = jnp.exp(b_last[:, :, :, 0, :])

    def step(S, inp):
        qd, ke, vc, dec = inp
        o = jnp.einsum('bhcd,bhdv->bhcv', qd, S)
        S = S * dec[..., None] + jnp.einsum('bhcd,bhcv->bhdv', ke, vc)
        return S, o

    xs = (jnp.moveaxis(q_d, 2, 0), jnp.moveaxis(k_end, 2, 0),
          jnp.moveaxis(v, 2, 0), jnp.moveaxis(decay, 2, 0))
    S0 = jnp.zeros((B, H, dk, dv), jnp.float32)
    _, o_inter = lax.scan(step, S0, xs)
    o = o_intra + jnp.moveaxis(o_inter, 0, 2)
    return o.reshape(B, H, L, dv)


def gla_mixer(h, w_in, w_gk1, w_gk2, b_gk, g_head, w_out):
    B, L, _ = h.shape
    proj = h @ w_in
    q, k, v, r = jnp.split(proj, [GLA_DK, 2 * GLA_DK, 2 * GLA_DK + GLA_DV], axis=-1)

    def heads(t, hd):
        return t.reshape(B, L, GLA_HEADS, hd).transpose(0, 2, 1, 3).astype(jnp.float32)

    q = heads(q, GLA_HK) * (GLA_HK ** -0.5)
    k = heads(k, GLA_HK)
    v = heads(v, GLA_HV)
    gate = jnp.einsum('bld,edr->eblr', h, w_gk1)
    gate = jnp.einsum('eblr,erk->eblk', gate, w_gk2) + b_gk[:, None, None, :]
    log_a = jax.nn.log_sigmoid(gate.astype(jnp.float32)) / GLA_TAU
    la_f = heads(log_a[0], GLA_HK)
    la_b = heads(log_a[1], GLA_HK)
    o_f = gla_direction(q, k, v, la_f, True)
    flip = lambda t: jnp.flip(t, axis=2)
    o_b = flip(gla_direction(flip(q), flip(k), flip(v), flip(la_b), False))
    o = rms_norm(o_f + o_b, g_head)
    o = o.transpose(0, 2, 1, 3).reshape(B, L, GLA_DV).astype(h.dtype)
    return (o * jax.nn.silu(r)) @ w_out


def sgu_mixer(h, w_in, b_in, ln_g, ln_b, w_s, b_s, w_out):
    B, L, _ = h.shape
    z = jax.nn.gelu(h @ w_in + b_in)
    u, v = jnp.split(z, 2, axis=-1)
    v = layer_norm(v, ln_g, ln_b)
    n = L // SG_CHUNK
    v = v.reshape(B, n, SG_CHUNK, SG_GROUPS, SG_GD)
    v = jnp.einsum('gts,bnsgd->bntgd', w_s, v) + b_s.T[None, None, :, :, None]
    v = v.reshape(B, L, SG_WIDTH)
    return (u * v) @ w_out


def trunk(x, c, norm_g, w_ada, b_ada,
          gla_w_in, gla_w_gk1, gla_w_gk2, gla_b_gk, gla_g_head, gla_w_out,
          sg_w_in, sg_b_in, sg_ln_g, sg_ln_b, sg_w_s, sg_b_s, sg_w_out,
          ffn_w_in, ffn_w_out):
    for i in range(DEPTH):
        mod = jax.nn.silu(c) @ w_ada[i] + b_ada[i]
        sh1, sc1, g1, sh2, sc2, g2 = jnp.split(mod[:, None, :], 6, axis=-1)
        h = rms_norm(x, norm_g[i, 0]) * (1 + sc1) + sh1
        j = i // N_MIXERS
        if i % N_MIXERS == 0:
            y = gla_mixer(h, gla_w_in[j], gla_w_gk1[j], gla_w_gk2[j], gla_b_gk[j],
                          gla_g_head[j], gla_w_out[j])
        else:
            y = sgu_mixer(h, sg_w_in[j], sg_b_in[j], sg_ln_g[j], sg_ln_b[j],
                          sg_w_s[j], sg_b_s[j], sg_w_out[j])
        x = x + g1 * rms_norm(y, norm_g[i, 1])
        h = rms_norm(x, norm_g[i, 2]) * (1 + sc2) + sh2
        a, bb = jnp.split(h @ ffn_w_in[i], 2, axis=-1)
        y = (jax.nn.silu(a) * bb) @ ffn_w_out[i]
        x = x + g2 * rms_norm(y, norm_g[i, 3])
    return x


def setup_inputs(seed: int = 0) -> dict:
    key = jax.random.key(seed)
    ks = jax.random.split(key, 24)
    D = D_MODEL
    nrm = lambda k, shape, s: jax.random.normal(k, shape, jnp.float32) * s
    return {
        'x_prompt': nrm(ks[0], (BATCH, SEQ, D), 1.0),
        'x_sample': nrm(ks[1], (DEC_BATCH, DEC_SEQ, D), 1.0),
        'c_prompt': nrm(ks[2], (BATCH, D), 1.0),
        'c_sample': nrm(ks[3], (DEC_BATCH, D), 1.0),
        'norm_g': 1.0 + nrm(ks[4], (DEPTH, 4, D), 0.05),
        'w_ada': nrm(ks[5], (DEPTH, D, 6 * D), 0.5 * D ** -0.5),
        'b_ada': nrm(ks[6], (DEPTH, 6 * D), 0.02),
        'gla_w_in': nrm(ks[7], (N_GLA, D, 2 * GLA_DK + 2 * GLA_DV), D ** -0.5),
        'gla_w_gk1': nrm(ks[8], (N_GLA, 2, D, GLA_RANK), D ** -0.5),
        'gla_w_gk2': nrm(ks[9], (N_GLA, 2, GLA_RANK, GLA_DK), GLA_RANK ** -0.5),
        'gla_b_gk': nrm(ks[10], (N_GLA, 2, GLA_DK), 0.1),
        'gla_g_head': 1.0 + nrm(ks[11], (N_GLA, GLA_HV), 0.05),
        'gla_w_out': nrm(ks[12], (N_GLA, GLA_DV, D), GLA_DV ** -0.5),
        'sg_w_in': nrm(ks[13], (N_SG, D, 2 * SG_WIDTH), D ** -0.5),
        'sg_b_in': nrm(ks[14], (N_SG, 2 * SG_WIDTH), 0.02),
        'sg_ln_g': 1.0 + nrm(ks[15], (N_SG, SG_WIDTH), 0.05),
        'sg_ln_b': nrm(ks[16], (N_SG, SG_WIDTH), 0.02),
        'sg_w_s': nrm(ks[17], (N_SG, SG_GROUPS, SG_CHUNK, SG_CHUNK), SG_CHUNK ** -0.5),
        'sg_b_s': 1.0 + nrm(ks[18], (N_SG, SG_GROUPS, SG_CHUNK), 0.1),
        'sg_w_out': nrm(ks[19], (N_SG, SG_WIDTH, D), SG_WIDTH ** -0.5),
        'ffn_w_in': nrm(ks[20], (DEPTH, D, 2 * FFN_HIDDEN), D ** -0.5),
        'ffn_w_out': nrm(ks[21], (DEPTH, FFN_HIDDEN, D), FFN_HIDDEN ** -0.5),
    }


def reference(x_prompt, x_sample, c_prompt, c_sample, norm_g, w_ada, b_ada,
              gla_w_in, gla_w_gk1, gla_w_gk2, gla_b_gk, gla_g_head, gla_w_out,
              sg_w_in, sg_b_in, sg_ln_g, sg_ln_b, sg_w_s, sg_b_s, sg_w_out,
              ffn_w_in, ffn_w_out):
    y_prompt = trunk(x_prompt, c_prompt, norm_g, w_ada, b_ada,
                     gla_w_in, gla_w_gk1, gla_w_gk2, gla_b_gk, gla_g_head, gla_w_out,
                     sg_w_in, sg_b_in, sg_ln_g, sg_ln_b, sg_w_s, sg_b_s, sg_w_out,
                     ffn_w_in, ffn_w_out)
    y_sample = trunk(x_sample, c_sample, norm_g, w_ada, b_ada,
                     gla_w_in, gla_w_gk1, gla_w_gk2, gla_b_gk, gla_g_head, gla_w_out,
                     sg_w_in, sg_b_in, sg_ln_g, sg_ln_b, sg_w_s, sg_b_s, sg_w_out,
                     ffn_w_in, ffn_w_out)
    return (y_prompt, y_sample)
```

```python
import functools

import jax
import jax.numpy as jnp
from jax import lax
from jax.experimental import pallas as pl
from jax.experimental.pallas import tpu as pltpu

D = 1024
DEPTH = 4
N_HEADS = 4
DK = 512
DV = 1024
HK = DK // N_HEADS
HV = DV // N_HEADS
RANK = 16
RANK_PAD = 128
TAU = 16.0
CHUNK = 64
SG_CHUNK = 128
SG_GROUPS = 4
SG_WIDTH = 1024
SG_GD = SG_WIDTH // SG_GROUPS
FFN_HIDDEN = 2816
FFN_CHUNK = 256
N_FFN_CHUNKS = FFN_HIDDEN // FFN_CHUNK
EPS = 1e-6

GLA_TILE = 256
SGU_TILE = 256
VMEM_LIMIT = 56 * 1024 * 1024

F32 = jnp.float32
BF16 = jnp.bfloat16


def _dot(a, b):
    return jnp.dot(a, b, preferred_element_type=F32)


def _dot_nt(a, b):
    return lax.dot_general(a, b, (((1,), (1,)), ((), ())), preferred_element_type=F32)


def _dot_tn(a, b):
    return lax.dot_general(a, b, (((0,), (0,)), ((), ())), preferred_element_type=F32)


def _sigmoid(x):
    return 1.0 / (1.0 + jnp.exp(-x))


def _rms(x, g):
    ms = jnp.mean(x * x, axis=-1, keepdims=True)
    return x * lax.rsqrt(ms + EPS) * g


def _log_sigmoid(x):
    return jnp.minimum(x, 0.0) - jnp.log1p(jnp.exp(-jnp.abs(x)))


def _split_bf16(x):
    hi = x.astype(BF16)
    lo = (x - hi.astype(F32)).astype(BF16)
    return hi, lo


def _const_spec(shape):
    zeros = (0,) * len(shape)
    return pl.BlockSpec(shape, lambda *_: zeros, pipeline_mode=pl.Buffered(1))


def _ada_kernel(c_ref, w_ref, b_ref, o_ref):
    c = c_ref[...]
    s = (c * _sigmoid(c)).astype(BF16)
    o_ref[0] = _dot(s, w_ref[0].astype(BF16)) + b_ref[0]


def _ada_call(c_all, w_ada, b_ada):
    bp = c_all.shape[0]
    return pl.pallas_call(
        _ada_kernel,
        grid=(DEPTH, 6),
        in_specs=[
            pl.BlockSpec((bp, D), lambda i, j: (0, 0)),
            pl.BlockSpec((1, D, D), lambda i, j: (i, 0, j)),
            pl.BlockSpec((1, 1, D), lambda i, j: (i, 0, j)),
        ],
        out_specs=pl.BlockSpec((1, bp, D), lambda i, j: (i, 0, j)),
        out_shape=jax.ShapeDtypeStruct((DEPTH, bp, 6 * D), F32),
        name="ada_mod",
    )(c_all, w_ada, b_ada.reshape(DEPTH, 1, 6 * D))


def _ffn_residual(x1, mod_ref, ng_ref, wa_ref, wb_ref, wo_ref):
    sh2 = mod_ref[0, :, 3 * D:4 * D]
    sc2 = mod_ref[0, :, 4 * D:5 * D]
    g2 = mod_ref[0, :, 5 * D:6 * D]
    hb = (_rms(x1, ng_ref[2:3, :]) * (1.0 + sc2) + sh2).astype(BF16)
    acc = None
    for c in range(N_FFN_CHUNKS):
        a = _dot(hb, wa_ref[c])
        b = _dot(hb, wb_ref[c])
        act = (a * _sigmoid(a) * b).astype(BF16)
        y = _dot(act, wo_ref[c])
        acc = y if acc is None else acc + y
    return x1 + g2 * _rms(acc, ng_ref[3:4, :])


def _chunk_decay_columns(rows):
    pad = jnp.zeros((128 - len(rows), 128), F32)
    m = jnp.concatenate(rows + [pad], axis=0)
    return jnp.exp(m.T)


def _gla_scan_tile(q, k, v_bf, la, s_ref, out_fn, *, reverse):
    tm = q.shape[0]
    nc = tm // CHUNK
    row = lax.broadcasted_iota(jnp.int32, (tm, tm), 0)
    col = lax.broadcasted_iota(jnp.int32, (tm, tm), 1)
    same_chunk = (row // CHUNK) == (col // CHUNK)
    tri = (col >= row) if reverse else (col <= row)
    t_blk = jnp.where(same_chunk & tri, 1.0, 0.0).astype(BF16)
    la_hi, la_lo = _split_bf16(la)
    b = _dot(t_blk, la_hi) + _dot(t_blk, la_lo)

    q_d = (q * jnp.exp(b)).astype(BF16)
    k_d = (k * jnp.exp(-b)).astype(BF16)

    def edge(c):
        r0 = c * CHUNK if reverse else c * CHUNK + CHUNK - 1
        return b[r0:r0 + 1, :]

    dec = _chunk_decay_columns(
        [edge(c)[:, h * HK:(h + 1) * HK] for c in range(nc) for h in range(N_HEADS)])

    ri = lax.broadcasted_iota(jnp.int32, (CHUNK, CHUNK), 0)
    ci = lax.broadcasted_iota(jnp.int32, (CHUNK, CHUNK), 1)
    mask = (ci > ri) if reverse else (ci <= ri)

    order = range(nc - 1, -1, -1) if reverse else range(nc)
    for c in order:
        rs = slice(c * CHUNK, (c + 1) * CHUNK)
        k_e = (k[rs, :] * jnp.exp(edge(c) - b[rs, :])).astype(BF16)
        for h in range(N_HEADS):
            ks = slice(h * HK, (h + 1) * HK)
            vs = slice(h * HV, (h + 1) * HV)
            qd = q_d[rs, ks]
            vv = v_bf[rs, vs]
            sc = _dot_nt(qd, k_d[rs, ks])
            sc = jnp.where(mask, sc, 0.0).astype(BF16)
            s_old = s_ref[h]
            o = _dot(sc, vv) + _dot(qd, s_old.astype(BF16))
            out_fn(c, h, o)
            j = c * N_HEADS + h
            s_ref[h] = s_old * dec[:, j:j + 1] + _dot_tn(k_e[:, ks], vv)


def _gla_a_kernel(x_ref, mod_ref, ng_ref, win_ref, wg1_ref, wg2_ref, bg_ref,
                  q_ref, k_ref, v_ref, r_ref, g1b_ref, of_ref, s_ref):
    @pl.when(pl.program_id(1) == 0)
    def _():
        s_ref[...] = jnp.zeros_like(s_ref)

    x = x_ref[0]
    sh1 = mod_ref[0, :, 0:D]
    sc1 = mod_ref[0, :, D:2 * D]
    hb = (_rms(x, ng_ref[0:1, :]) * (1.0 + sc1) + sh1).astype(BF16)

    proj = _dot(hb, win_ref[...])
    q = proj[:, 0:DK] * (HK ** -0.5)
    k = proj[:, DK:2 * DK]
    v_bf = proj[:, 2 * DK:2 * DK + DV].astype(BF16)
    q_ref[0] = q.astype(BF16)
    k_ref[0] = k.astype(BF16)
    v_ref[0] = v_bf
    r_ref[0] = proj[:, 2 * DK + DV:].astype(BF16)

    g1 = _dot(hb, wg1_ref[...])
    g1b_ref[0] = g1[:, RANK_PAD:].astype(BF16)
    gate = _dot(g1[:, :RANK_PAD].astype(BF16), wg2_ref[0]) + bg_ref[0:1, :]
    la = _log_sigmoid(gate) * (1.0 / TAU)

    def put(c, h, o):
        of_ref[0, c * CHUNK:(c + 1) * CHUNK, h * HV:(h + 1) * HV] = o.astype(BF16)

    _gla_scan_tile(q, k, v_bf, la, s_ref, put, reverse=False)


def _gla_a_call(x, mod_l, ng_l, w_in, wg1, wg2, bg):
    B, L, _ = x.shape
    tm = GLA_TILE
    nt = L // tm
    tok = lambda w: pl.BlockSpec((1, tm, w), lambda b, t: (b, t, 0))
    out_shapes = (
        jax.ShapeDtypeStruct((B, L, DK), BF16),
        jax.ShapeDtypeStruct((B, L, DK), BF16),
        jax.ShapeDtypeStruct((B, L, DV), BF16),
        jax.ShapeDtypeStruct((B, L, DV), BF16),
        jax.ShapeDtypeStruct((B, L, RANK_PAD), BF16),
        jax.ShapeDtypeStruct((B, L, DV), BF16),
    )
    return pl.pallas_call(
        _gla_a_kernel,
        grid=(B, nt),
        in_specs=[
            tok(D),
            pl.BlockSpec((1, 1, 6 * D), lambda b, t: (b, 0, 0)),
            _const_spec((4, D)),
            _const_spec((D, 2 * DK + 2 * DV)),
            _const_spec((D, 2 * RANK_PAD)),
            _const_spec((2, RANK_PAD, DK)),
            _const_spec((2, DK)),
        ],
        out_specs=(tok(DK), tok(DK), tok(DV), tok(DV), tok(RANK_PAD), tok(DV)),
        out_shape=out_shapes,
        scratch_shapes=[pltpu.VMEM((N_HEADS, HK, HV), F32)],
        compiler_params=pltpu.CompilerParams(
            dimension_semantics=("arbitrary", "arbitrary"), vmem_limit_bytes=VMEM_LIMIT),
        name="gla_a",
    )(x, mod_l, ng_l, w_in, wg1, wg2, bg)


def _gla_b_kernel(x_ref, mod_ref, ng_ref, q_ref, k_ref, v_ref, r_ref, g1b_ref, of_ref,
                  wg2_ref, bg_ref, gh_ref, wout_ref, wa_ref, wb_ref, wo_ref,
                  y_ref, s_ref, o_scr):
    @pl.when(pl.program_id(1) == 0)
    def _():
        s_ref[...] = jnp.zeros_like(s_ref)

    gate = _dot(g1b_ref[0], wg2_ref[1]) + bg_ref[1:2, :]
    la = _log_sigmoid(gate) * (1.0 / TAU)
    q = q_ref[0].astype(F32)
    k = k_ref[0].astype(F32)

    def put(c, h, o):
        rs = slice(c * CHUNK, (c + 1) * CHUNK)
        vs = slice(h * HV, (h + 1) * HV)
        o_scr[rs, vs] = o + of_ref[0, rs, vs].astype(F32)

    _gla_scan_tile(q, k, v_ref[0], la, s_ref, put, reverse=True)

    gh = gh_ref[...]
    r = r_ref[0].astype(F32)
    gated = []
    for h in range(N_HEADS):
        vs = slice(h * HV, (h + 1) * HV)
        rh = r[:, vs]
        gated.append((_rms(o_scr[:, vs], gh) * (rh * _sigmoid(rh))).astype(BF16))
    y = _dot(jnp.concatenate(gated, axis=-1), wout_ref[...])

    g1 = mod_ref[0, :, 2 * D:3 * D]
    x1 = x_ref[0] + g1 * _rms(y, ng_ref[1:2, :])
    y_ref[0] = _ffn_residual(x1, mod_ref, ng_ref, wa_ref, wb_ref, wo_ref)


def _gla_b_call(x, mod_l, ng_l, q, k, v, r, g1b, o_f, wg2, bg, g_head, w_out, wa, wb, wo):
    B, L, _ = x.shape
    tm = GLA_TILE
    nt = L // tm
    tok = lambda w: pl.BlockSpec((1, tm, w), lambda b, t: (b, nt - 1 - t, 0))
    return pl.pallas_call(
        _gla_b_kernel,
        grid=(B, nt),
        in_specs=[
            tok(D),
            pl.BlockSpec((1, 1, 6 * D), lambda b, t: (b, 0, 0)),
            _const_spec((4, D)),
            tok(DK), tok(DK), tok(DV), tok(DV), tok(RANK_PAD), tok(DV),
            _const_spec((2, RANK_PAD, DK)),
            _const_spec((2, DK)),
            _const_spec((1, HV)),
            _const_spec((DV, D)),
            _const_spec((N_FFN_CHUNKS, D, FFN_CHUNK)),
            _const_spec((N_FFN_CHUNKS, D, FFN_CHUNK)),
            _const_spec((N_FFN_CHUNKS, FFN_CHUNK, D)),
        ],
        out_specs=tok(D),
        out_shape=jax.ShapeDtypeStruct((B, L, D), F32),
        scratch_shapes=[pltpu.VMEM((N_HEADS, HK, HV), F32), pltpu.VMEM((tm, DV), F32)],
        compiler_params=pltpu.CompilerParams(
            dimension_semantics=("arbitrary", "arbitrary"), vmem_limit_bytes=VMEM_LIMIT),
        name="gla_b",
    )(x, mod_l, ng_l, q, k, v, r, g1b, o_f, wg2, bg, g_head, w_out, wa, wb, wo)


def _gelu_tanh(x):
    c = 0.7978845608028654
    return 0.5 * x * (1.0 + jnp.tanh(c * (x + 0.044715 * (x * x * x))))


def _sgu_kernel(x_ref, mod_ref, ng_ref, win_ref, bin_ref, lng_ref, lnb_ref, ws_ref, bs_ref,
                wout_ref, wa_ref, wb_ref, wo_ref, y_ref, m_scr):
    tm = x_ref.shape[1]
    x = x_ref[0]
    sh1 = mod_ref[0, :, 0:D]
    sc1 = mod_ref[0, :, D:2 * D]
    hb = (_rms(x, ng_ref[0:1, :]) * (1.0 + sc1) + sh1).astype(BF16)
    z = _gelu_tanh(_dot(hb, win_ref[...]) + bin_ref[...])
    u = z[:, :SG_WIDTH]
    v = z[:, SG_WIDTH:]
    mu = jnp.mean(v, axis=-1, keepdims=True)
    vc = v - mu
    var = jnp.mean(vc * vc, axis=-1, keepdims=True)
    vn = (vc * lax.rsqrt(var + EPS) * lng_ref[...] + lnb_ref[...]).astype(BF16)
    for n in range(tm // SG_CHUNK):
        rs = slice(n * SG_CHUNK, (n + 1) * SG_CHUNK)
        for g in range(SG_GROUPS):
            cs = slice(g * SG_GD, (g + 1) * SG_GD)
            mixed = _dot(ws_ref[g], vn[rs, cs]) + bs_ref[:, cs]
            m_scr[rs, cs] = (u[rs, cs] * mixed).astype(BF16)
    y = _dot(m_scr[...], wout_ref[...])
    g1 = mod_ref[0, :, 2 * D:3 * D]
    x1 = x + g1 * _rms(y, ng_ref[1:2, :])
    y_ref[0] = _ffn_residual(x1, mod_ref, ng_ref, wa_ref, wb_ref, wo_ref)


def _sgu_call(x, mod_l, ng_l, w_in, b_in, ln_g, ln_b, w_s, bs_full, w_out, wa, wb, wo):
    B, L, _ = x.shape
    tm = SGU_TILE
    nt = L // tm
    tok = pl.BlockSpec((1, tm, D), lambda b, t: (b, t, 0))
    return pl.pallas_call(
        _sgu_kernel,
        grid=(B, nt),
        in_specs=[
            tok,
            pl.BlockSpec((1, 1, 6 * D), lambda b, t: (b, 0, 0)),
            _const_spec((4, D)),
            _const_spec((D, 2 * SG_WIDTH)),
            _const_spec((1, 2 * SG_WIDTH)),
            _const_spec((1, SG_WIDTH)),
            _const_spec((1, SG_WIDTH)),
            _const_spec((SG_GROUPS, SG_CHUNK, SG_CHUNK)),
            _const_spec((SG_CHUNK, SG_WIDTH)),
            _const_spec((SG_WIDTH, D)),
            _const_spec((N_FFN_CHUNKS, D, FFN_CHUNK)),
            _const_spec((N_FFN_CHUNKS, D, FFN_CHUNK)),
            _const_spec((N_FFN_CHUNKS, FFN_CHUNK, D)),
        ],
        out_specs=tok,
        out_shape=jax.ShapeDtypeStruct((B, L, D), F32),
        scratch_shapes=[pltpu.VMEM((tm, SG_WIDTH), BF16)],
        compiler_params=pltpu.CompilerParams(
            dimension_semantics=("arbitrary", "arbitrary"), vmem_limit_bytes=VMEM_LIMIT),
        name="sgu_layer",
    )(x, mod_l, ng_l, w_in, b_in, ln_g, ln_b, w_s, bs_full, w_out, wa, wb, wo)


def _prep_ffn(ffn_w_in, ffn_w_out, i):
    w = ffn_w_in[i].astype(BF16)
    wa = w[:, :FFN_HIDDEN].reshape(D, N_FFN_CHUNKS, FFN_CHUNK).transpose(1, 0, 2)
    wb = w[:, FFN_HIDDEN:].reshape(D, N_FFN_CHUNKS, FFN_CHUNK).transpose(1, 0, 2)
    wo = ffn_w_out[i].astype(BF16).reshape(N_FFN_CHUNKS, FFN_CHUNK, D)
    return wa, wb, wo


def _prep_gate(gla_w_gk1, gla_w_gk2, j):
    wg1 = jnp.zeros((D, 2 * RANK_PAD), BF16)
    wg1 = wg1.at[:, :RANK].set(gla_w_gk1[j, 0].astype(BF16))
    wg1 = wg1.at[:, RANK_PAD:RANK_PAD + RANK].set(gla_w_gk1[j, 1].astype(BF16))
    wg2 = jnp.zeros((2, RANK_PAD, DK), BF16).at[:, :RANK, :].set(gla_w_gk2[j].astype(BF16))
    return wg1, wg2


def _trunk(x, mod, norm_g, gla_w_in, gla_w_gk1, gla_w_gk2, gla_b_gk, gla_g_head, gla_w_out,
           sg_w_in, sg_b_in, sg_ln_g, sg_ln_b, sg_w_s, sg_b_s, sg_w_out, ffn):
    B = x.shape[0]
    for i in range(DEPTH):
        mod_l = mod[i].reshape(B, 1, 6 * D)
        wa, wb, wo = ffn[i]
        j = i // 2
        if i % 2 == 0:
            wg1, wg2 = _prep_gate(gla_w_gk1, gla_w_gk2, j)
            q, k, v, r, g1b, o_f = _gla_a_call(
                x, mod_l, norm_g[i], gla_w_in[j].astype(BF16), wg1, wg2, gla_b_gk[j])
            x = _gla_b_call(x, mod_l, norm_g[i], q, k, v, r, g1b, o_f, wg2, gla_b_gk[j],
                            gla_g_head[j].reshape(1, HV), gla_w_out[j].astype(BF16), wa, wb, wo)
        else:
            bs_full = jnp.repeat(sg_b_s[j].T, SG_GD, axis=1)
            x = _sgu_call(x, mod_l, norm_g[i], sg_w_in[j].astype(BF16),
                          sg_b_in[j].reshape(1, -1), sg_ln_g[j].reshape(1, -1),
                          sg_ln_b[j].reshape(1, -1), sg_w_s[j].astype(BF16), bs_full,
                          sg_w_out[j].astype(BF16), wa, wb, wo)
    return x


def kernel(x_prompt, x_sample, c_prompt, c_sample, norm_g, w_ada, b_ada, gla_w_in, gla_w_gk1,
           gla_w_gk2, gla_b_gk, gla_g_head, gla_w_out, sg_w_in, sg_b_in, sg_ln_g, sg_ln_b,
           sg_w_s, sg_b_s, sg_w_out, ffn_w_in, ffn_w_out):
    bp, bs = c_prompt.shape[0], c_sample.shape[0]
    rows = -(-(bp + bs) // 16) * 16
    c_all = jnp.concatenate([c_prompt, c_sample, jnp.zeros((rows - bp - bs, D), F32)], axis=0)
    mod = _ada_call(c_all, w_ada, b_ada)
    ffn = [_prep_ffn(ffn_w_in, ffn_w_out, i) for i in range(DEPTH)]
    args = (norm_g, gla_w_in, gla_w_gk1, gla_w_gk2, gla_b_gk, gla_g_head, gla_w_out,
            sg_w_in, sg_b_in, sg_ln_g, sg_ln_b, sg_w_s, sg_b_s, sg_w_out, ffn)
    y_prompt = _trunk(x_prompt, mod[:, :bp], *args)
    y_sample = _trunk(x_sample, mod[:, bp:bp + bs], *args)
    return (y_prompt, y_sample)
```

```python
import functools

import jax
import jax.numpy as jnp
from jax import lax
from jax.experimental import pallas as pl
from jax.experimental.pallas import tpu as pltpu

D = 1024
DEPTH = 4
N_HEADS = 4
DK = 512
DV = 1024
HK = DK // N_HEADS
HV = DV // N_HEADS
RANK = 16
RANK_PAD = 128
TAU = 16.0
CHUNK = 64
SG_CHUNK = 128
SG_GROUPS = 4
SG_WIDTH = 1024
SG_GD = SG_WIDTH // SG_GROUPS
FFN_HIDDEN = 2816
FFN_CHUNK = 256
N_FFN_CHUNKS = FFN_HIDDEN // FFN_CHUNK
EPS = 1e-6

GLA_TILE = 256
SGU_TILE = 256
VMEM_LIMIT = 56 * 1024 * 1024

F32 = jnp.float32
BF16 = jnp.bfloat16


def _dot(a, b):
    return jnp.dot(a, b, preferred_element_type=F32)


def _dot_nt(a, b):
    return lax.dot_general(a, b, (((1,), (1,)), ((), ())), preferred_element_type=F32)


def _dot_tn(a, b):
    return lax.dot_general(a, b, (((0,), (0,)), ((), ())), preferred_element_type=F32)


def _sigmoid(x):
    return 1.0 / (1.0 + jnp.exp(-x))


def _rms(x, g):
    ms = jnp.mean(x * x, axis=-1, keepdims=True)
    return x * lax.rsqrt(ms + EPS) * g


def _log_sigmoid(x):
    return jnp.minimum(x, 0.0) - jnp.log1p(jnp.exp(-jnp.abs(x)))


def _split_bf16(x):
    hi = x.astype(BF16)
    lo = (x - hi.astype(F32)).astype(BF16)
    return hi, lo


def _const_spec(shape):
    zeros = (0,) * len(shape)
    return pl.BlockSpec(shape, lambda *_: zeros, pipeline_mode=pl.Buffered(1))


def _ada_kernel(c_ref, w_ref, b_ref, o_ref):
    c = c_ref[...]
    s = (c * _sigmoid(c)).astype(BF16)
    o_ref[0] = _dot(s, w_ref[0].astype(BF16)) + b_ref[0]


def _ada_call(c_all, w_ada, b_ada):
    bp = c_all.shape[0]
    return pl.pallas_call(
        _ada_kernel,
        grid=(DEPTH, 6),
        in_specs=[
            pl.BlockSpec((bp, D), lambda i, j: (0, 0)),
            pl.BlockSpec((1, D, D), lambda i, j: (i, 0, j)),
            pl.BlockSpec((1, 1, D), lambda i, j: (i, 0, j)),
        ],
        out_specs=pl.BlockSpec((1, bp, D), lambda i, j: (i, 0, j)),
        out_shape=jax.ShapeDtypeStruct((DEPTH, bp, 6 * D), F32),
        name="ada_mod",
    )(c_all, w_ada, b_ada.reshape(DEPTH, 1, 6 * D))


def _ffn_steps(x1, mod_ref, ng_ref, wa_ref, wb_ref, wo_ref, out_fn):
    sh2 = mod_ref[0, :, 3 * D:4 * D]
    sc2 = mod_ref[0, :, 4 * D:5 * D]
    g2 = mod_ref[0, :, 5 * D:6 * D]
    hb = (_rms(x1, ng_ref[2:3, :]) * (1.0 + sc2) + sh2).astype(BF16)
    yield
    up = lambda c: (_dot(hb, wa_ref[c]), _dot(hb, wb_ref[c]))
    acc = None
    nxt = up(0)
    for c in range(N_FFN_CHUNKS):
        a, b = nxt
        if c + 1 < N_FFN_CHUNKS:
            nxt = up(c + 1)
        act = (a * _sigmoid(a) * b).astype(BF16)
        y = _dot(act, wo_ref[c])
        acc = y if acc is None else acc + y
        yield
    out_fn(x1 + g2 * _rms(acc, ng_ref[3:4, :]))


def _run(gen):
    for _ in gen:
        pass


def _interleave(*gens):
    live = list(gens)
    while live:
        for g in list(live):
            try:
                next(g)
            except StopIteration:
                live.remove(g)


def _ffn_residual(x1, mod_ref, ng_ref, wa_ref, wb_ref, wo_ref):
    out = []
    _run(_ffn_steps(x1, mod_ref, ng_ref, wa_ref, wb_ref, wo_ref, out.append))
    return out[0]


def _chunk_decay_columns(rows):
    pad = jnp.zeros((128 - len(rows), 128), F32)
    m = jnp.concatenate(rows + [pad], axis=0)
    return jnp.exp(m.T)


def _gla_scan_tile(q, k, v_bf, la, s_ref, out_fn, *, reverse):
    tm = q.shape[0]
    nc = tm // CHUNK
    row = lax.broadcasted_iota(jnp.int32, (tm, tm), 0)
    col = lax.broadcasted_iota(jnp.int32, (tm, tm), 1)
    same_chunk = (row // CHUNK) == (col // CHUNK)
    tri = (col >= row) if reverse else (col <= row)
    t_blk = jnp.where(same_chunk & tri, 1.0, 0.0).astype(BF16)
    la_hi, la_lo = _split_bf16(la)
    b = _dot(t_blk, la_hi) + _dot(t_blk, la_lo)

    q_d = (q * jnp.exp(b)).astype(BF16)
    k_d = (k * jnp.exp(-b)).astype(BF16)

    def edge(c):
        r0 = c * CHUNK if reverse else c * CHUNK + CHUNK - 1
        return b[r0:r0 + 1, :]

    dec = _chunk_decay_columns(
        [edge(c)[:, h * HK:(h + 1) * HK] for c in range(nc) for h in range(N_HEADS)])

    ri = lax.broadcasted_iota(jnp.int32, (CHUNK, CHUNK), 0)
    ci = lax.broadcasted_iota(jnp.int32, (CHUNK, CHUNK), 1)
    mask = (ci > ri) if reverse else (ci <= ri)

    order = range(nc - 1, -1, -1) if reverse else range(nc)
    for c in order:
        rs = slice(c * CHUNK, (c + 1) * CHUNK)
        k_e = (k[rs, :] * jnp.exp(edge(c) - b[rs, :])).astype(BF16)
        for h in range(N_HEADS):
            ks = slice(h * HK, (h + 1) * HK)
            vs = slice(h * HV, (h + 1) * HV)
            qd = q_d[rs, ks]
            vv = v_bf[rs, vs]
            sc = _dot_nt(qd, k_d[rs, ks])
            sc = jnp.where(mask, sc, 0.0).astype(BF16)
            s_old = s_ref[h]
            o = _dot(sc, vv) + _dot(qd, s_old.astype(BF16))
            out_fn(c, h, o)
            j = c * N_HEADS + h
            s_ref[h] = s_old * dec[:, j:j + 1] + _dot_tn(k_e[:, ks], vv)


def _gla_a_kernel(x_ref, mod_ref, ng_ref, win_ref, wg1_ref, wg2_ref, bg_ref,
                  q_ref, k_ref, v_ref, r_ref, g1b_ref, of_ref, s_ref):
    @pl.when(pl.program_id(1) == 0)
    def _():
        s_ref[...] = jnp.zeros_like(s_ref)

    x = x_ref[0]
    sh1 = mod_ref[0, :, 0:D]
    sc1 = mod_ref[0, :, D:2 * D]
    hb = (_rms(x, ng_ref[0:1, :]) * (1.0 + sc1) + sh1).astype(BF16)

    proj = _dot(hb, win_ref[...])
    q = proj[:, 0:DK] * (HK ** -0.5)
    k = proj[:, DK:2 * DK]
    v_bf = proj[:, 2 * DK:2 * DK + DV].astype(BF16)
    q_ref[0] = q.astype(BF16)
    k_ref[0] = k.astype(BF16)
    v_ref[0] = v_bf
    r_ref[0] = proj[:, 2 * DK + DV:].astype(BF16)

    g1 = _dot(hb, wg1_ref[...])
    g1b_ref[0] = g1[:, RANK_PAD:].astype(BF16)
    gate = _dot(g1[:, :RANK_PAD].astype(BF16), wg2_ref[0]) + bg_ref[0:1, :]
    la = _log_sigmoid(gate) * (1.0 / TAU)

    def put(c, h, o):
        of_ref[0, c * CHUNK:(c + 1) * CHUNK, h * HV:(h + 1) * HV] = o.astype(BF16)

    _gla_scan_tile(q, k, v_bf, la, s_ref, put, reverse=False)


def _gla_a_call(x, mod_l, ng_l, w_in, wg1, wg2, bg):
    B, L, _ = x.shape
    tm = GLA_TILE
    nt = L // tm
    tok = lambda w: pl.BlockSpec((1, tm, w), lambda b, t: (b, t, 0))
    out_shapes = (
        jax.ShapeDtypeStruct((B, L, DK), BF16),
        jax.ShapeDtypeStruct((B, L, DK), BF16),
        jax.ShapeDtypeStruct((B, L, DV), BF16),
        jax.ShapeDtypeStruct((B, L, DV), BF16),
        jax.ShapeDtypeStruct((B, L, RANK_PAD), BF16),
        jax.ShapeDtypeStruct((B, L, DV), BF16),
    )
    return pl.pallas_call(
        _gla_a_kernel,
        grid=(B, nt),
        in_specs=[
            tok(D),
            pl.BlockSpec((1, 1, 6 * D), lambda b, t: (b, 0, 0)),
            _const_spec((4, D)),
            _const_spec((D, 2 * DK + 2 * DV)),
            _const_spec((D, 2 * RANK_PAD)),
            _const_spec((2, RANK_PAD, DK)),
            _const_spec((2, DK)),
        ],
        out_specs=(tok(DK), tok(DK), tok(DV), tok(DV), tok(RANK_PAD), tok(DV)),
        out_shape=out_shapes,
        scratch_shapes=[pltpu.VMEM((N_HEADS, HK, HV), F32)],
        compiler_params=pltpu.CompilerParams(
            dimension_semantics=("arbitrary", "arbitrary"), vmem_limit_bytes=VMEM_LIMIT),
        name="gla_a",
    )(x, mod_l, ng_l, w_in, wg1, wg2, bg)


def _gla_b_kernel(x_ref, mod_ref, ng_ref, q_ref, k_ref, v_ref, r_ref, g1b_ref, of_ref,
                  wg2_ref, bg_ref, gh_ref, wout_ref, wa_ref, wb_ref, wo_ref,
                  y_ref, s_ref, o_scr):
    @pl.when(pl.program_id(1) == 0)
    def _():
        s_ref[...] = jnp.zeros_like(s_ref)

    gate = _dot(g1b_ref[0], wg2_ref[1]) + bg_ref[1:2, :]
    la = _log_sigmoid(gate) * (1.0 / TAU)
    q = q_ref[0].astype(F32)
    k = k_ref[0].astype(F32)

    def put(c, h, o):
        rs = slice(c * CHUNK, (c + 1) * CHUNK)
        vs = slice(h * HV, (h + 1) * HV)
        o_scr[rs, vs] = o + of_ref[0, rs, vs].astype(F32)

    _gla_scan_tile(q, k, v_ref[0], la, s_ref, put, reverse=True)

    gh = gh_ref[...]
    r = r_ref[0].astype(F32)
    gated = []
    for h in range(N_HEADS):
        vs = slice(h * HV, (h + 1) * HV)
        rh = r[:, vs]
        gated.append((_rms(o_scr[:, vs], gh) * (rh * _sigmoid(rh))).astype(BF16))
    y = _dot(jnp.concatenate(gated, axis=-1), wout_ref[...])

    g1 = mod_ref[0, :, 2 * D:3 * D]
    x1 = x_ref[0] + g1 * _rms(y, ng_ref[1:2, :])
    y_ref[0] = _ffn_residual(x1, mod_ref, ng_ref, wa_ref, wb_ref, wo_ref)


def _gla_b_call(x, mod_l, ng_l, q, k, v, r, g1b, o_f, wg2, bg, g_head, w_out, wa, wb, wo):
    B, L, _ = x.shape
    tm = GLA_TILE
    nt = L // tm
    tok = lambda w: pl.BlockSpec((1, tm, w), lambda b, t: (b, nt - 1 - t, 0))
    return pl.pallas_call(
        _gla_b_kernel,
        grid=(B, nt),
        in_specs=[
            tok(D),
            pl.BlockSpec((1, 1, 6 * D), lambda b, t: (b, 0, 0)),
            _const_spec((4, D)),
            tok(DK), tok(DK), tok(DV), tok(DV), tok(RANK_PAD), tok(DV),
            _const_spec((2, RANK_PAD, DK)),
            _const_spec((2, DK)),
            _const_spec((1, HV)),
            _const_spec((DV, D)),
            _const_spec((N_FFN_CHUNKS, D, FFN_CHUNK)),
            _const_spec((N_FFN_CHUNKS, D, FFN_CHUNK)),
            _const_spec((N_FFN_CHUNKS, FFN_CHUNK, D)),
        ],
        out_specs=tok(D),
        out_shape=jax.ShapeDtypeStruct((B, L, D), F32),
        scratch_shapes=[pltpu.VMEM((N_HEADS, HK, HV), F32), pltpu.VMEM((tm, DV), F32)],
        compiler_params=pltpu.CompilerParams(
            dimension_semantics=("arbitrary", "arbitrary"), vmem_limit_bytes=VMEM_LIMIT),
        name="gla_b",
    )(x, mod_l, ng_l, q, k, v, r, g1b, o_f, wg2, bg, g_head, w_out, wa, wb, wo)


def _gelu_tanh(x):
    c = 0.7978845608028654
    return 0.5 * x * (1.0 + jnp.tanh(c * (x + 0.044715 * (x * x * x))))


def _sgu_kernel(x_ref, modm_ref, modf_ref, ng_ref, win_ref, bin_ref, lng_ref, lnb_ref, ws_ref,
                bs_ref, wout_ref, wa_ref, wb_ref, wo_ref, y_ref, m_scr, x1_scr):
    i = pl.program_id(0)
    tm = x_ref.shape[0]

    @pl.when(i == 0)
    def _():
        x1_scr[1] = jnp.zeros((tm, D), F32)

    def mixer():
        sh1 = modm_ref[0, :, 0:D]
        sc1 = modm_ref[0, :, D:2 * D]
        g1 = modm_ref[0, :, 2 * D:3 * D]
        x = x_ref[...]
        hb = (_rms(x, ng_ref[0:1, :]) * (1.0 + sc1) + sh1).astype(BF16)
        yield
        zs = []
        for j in range(4):
            cs = slice(j * 512, (j + 1) * 512)
            zs.append(_gelu_tanh(_dot(hb, win_ref[:, cs]) + bin_ref[:, cs]))
            yield
        v = jnp.concatenate(zs[2:], axis=-1)
        mu = jnp.mean(v, axis=-1, keepdims=True)
        vc = v - mu
        var = jnp.mean(vc * vc, axis=-1, keepdims=True)
        vn = (vc * lax.rsqrt(var + EPS) * lng_ref[...] + lnb_ref[...]).astype(BF16)
        yield
        for n in range(tm // SG_CHUNK):
            rs = slice(n * SG_CHUNK, (n + 1) * SG_CHUNK)
            for g in range(SG_GROUPS):
                cs = slice(g * SG_GD, (g + 1) * SG_GD)
                u = zs[g // 2][rs, (g % 2) * SG_GD:(g % 2 + 1) * SG_GD]
                mixed = _dot(ws_ref[g], vn[rs, cs]) + bs_ref[:, cs]
                m_scr[rs, cs] = (u * mixed).astype(BF16)
            yield
        ys = []
        for j in range(2):
            ys.append(_dot(m_scr[...], wout_ref[:, j * 512:(j + 1) * 512]))
            yield
        y = jnp.concatenate(ys, axis=-1)
        x1_scr[i % 2] = x + g1 * _rms(y, ng_ref[1:2, :])

    def put(val):
        y_ref[...] = val

    ffn = _ffn_steps(x1_scr[(i + 1) % 2], modf_ref, ng_ref, wa_ref, wb_ref, wo_ref, put)
    _interleave(ffn, mixer())


def _sgu_call(x, mod_l, ng_l, w_in, b_in, ln_g, ln_b, w_s, bs_full, w_out, wa, wb, wo):
    B, L, _ = x.shape
    tm = SGU_TILE
    n = (B * L) // tm
    per_seq = L // tm
    mix_tile = lambda i: jnp.minimum(i, n - 1)
    ffn_tile = lambda i: jnp.maximum(i - 1, 0)
    out = pl.pallas_call(
        _sgu_kernel,
        grid=(n + 1,),
        in_specs=[
            pl.BlockSpec((tm, D), lambda i: (mix_tile(i), 0)),
            pl.BlockSpec((1, 1, 6 * D), lambda i: (mix_tile(i) // per_seq, 0, 0)),
            pl.BlockSpec((1, 1, 6 * D), lambda i: (ffn_tile(i) // per_seq, 0, 0)),
            _const_spec((4, D)),
            _const_spec((D, 2 * SG_WIDTH)),
            _const_spec((1, 2 * SG_WIDTH)),
            _const_spec((1, SG_WIDTH)),
            _const_spec((1, SG_WIDTH)),
            _const_spec((SG_GROUPS, SG_CHUNK, SG_CHUNK)),
            _const_spec((SG_CHUNK, SG_WIDTH)),
            _const_spec((SG_WIDTH, D)),
            _const_spec((N_FFN_CHUNKS, D, FFN_CHUNK)),
            _const_spec((N_FFN_CHUNKS, D, FFN_CHUNK)),
            _const_spec((N_FFN_CHUNKS, FFN_CHUNK, D)),
        ],
        out_specs=pl.BlockSpec((tm, D), lambda i: (ffn_tile(i), 0)),
        out_shape=jax.ShapeDtypeStruct((B * L, D), F32),
        scratch_shapes=[pltpu.VMEM((tm, SG_WIDTH), BF16), pltpu.VMEM((2, tm, D), F32)],
        compiler_params=pltpu.CompilerParams(
            dimension_semantics=("arbitrary",), vmem_limit_bytes=VMEM_LIMIT),
        name="sgu_layer",
    )(x.reshape(B * L, D), mod_l, mod_l, ng_l, w_in, b_in, ln_g, ln_b, w_s, bs_full, w_out,
      wa, wb, wo)
    return out.reshape(B, L, D)


def _prep_ffn(ffn_w_in, ffn_w_out, i):
    w = ffn_w_in[i].astype(BF16)
    wa = w[:, :FFN_HIDDEN].reshape(D, N_FFN_CHUNKS, FFN_CHUNK).transpose(1, 0, 2)
    wb = w[:, FFN_HIDDEN:].reshape(D, N_FFN_CHUNKS, FFN_CHUNK).transpose(1, 0, 2)
    wo = ffn_w_out[i].astype(BF16).reshape(N_FFN_CHUNKS, FFN_CHUNK, D)
    return wa, wb, wo


def _prep_gate(gla_w_gk1, gla_w_gk2, j):
    wg1 = jnp.zeros((D, 2 * RANK_PAD), BF16)
    wg1 = wg1.at[:, :RANK].set(gla_w_gk1[j, 0].astype(BF16))
    wg1 = wg1.at[:, RANK_PAD:RANK_PAD + RANK].set(gla_w_gk1[j, 1].astype(BF16))
    wg2 = jnp.zeros((2, RANK_PAD, DK), BF16).at[:, :RANK, :].set(gla_w_gk2[j].astype(BF16))
    return wg1, wg2


def _trunk(x, mod, norm_g, gla_w_in, gla_w_gk1, gla_w_gk2, gla_b_gk, gla_g_head, gla_w_out,
           sg_w_in, sg_b_in, sg_ln_g, sg_ln_b, sg_w_s, sg_b_s, sg_w_out, ffn):
    B = x.shape[0]
    for i in range(DEPTH):
        mod_l = mod[i].reshape(B, 1, 6 * D)
        wa, wb, wo = ffn[i]
        j = i // 2
        if i % 2 == 0:
            wg1, wg2 = _prep_gate(gla_w_gk1, gla_w_gk2, j)
            q, k, v, r, g1b, o_f = _gla_a_call(
                x, mod_l, norm_g[i], gla_w_in[j].astype(BF16), wg1, wg2, gla_b_gk[j])
            x = _gla_b_call(x, mod_l, norm_g[i], q, k, v, r, g1b, o_f, wg2, gla_b_gk[j],
                            gla_g_head[j].reshape(1, HV), gla_w_out[j].astype(BF16), wa, wb, wo)
        else:
            bs_full = jnp.repeat(sg_b_s[j].T, SG_GD, axis=1)
            x = _sgu_call(x, mod_l, norm_g[i], sg_w_in[j].astype(BF16),
                          sg_b_in[j].reshape(1, -1), sg_ln_g[j].reshape(1, -1),
                          sg_ln_b[j].reshape(1, -1), sg_w_s[j].astype(BF16), bs_full,
                          sg_w_out[j].astype(BF16), wa, wb, wo)
    return x


def kernel(x_prompt, x_sample, c_prompt, c_sample, norm_g, w_ada, b_ada, gla_w_in, gla_w_gk1,
           gla_w_gk2, gla_b_gk, gla_g_head, gla_w_out, sg_w_in, sg_b_in, sg_ln_g, sg_ln_b,
           sg_w_s, sg_b_s, sg_w_out, ffn_w_in, ffn_w_out):
    bp, bs = c_prompt.shape[0], c_sample.shape[0]
    rows = -(-(bp + bs) // 16) * 16
    c_all = jnp.concatenate([c_prompt, c_sample, jnp.zeros((rows - bp - bs, D), F32)], axis=0)
    mod = _ada_call(c_all, w_ada, b_ada)
    ffn = [_prep_ffn(ffn_w_in, ffn_w_out, i) for i in range(DEPTH)]
    args = (norm_g, gla_w_in, gla_w_gk1, gla_w_gk2, gla_b_gk, gla_g_head, gla_w_out,
            sg_w_in, sg_b_in, sg_ln_g, sg_ln_b, sg_w_s, sg_b_s, sg_w_out, ffn)
    y_prompt = _trunk(x_prompt, mod[:, :bp], *args)
    y_sample = _trunk(x_sample, mod[:, bp:bp + bs], *args)
    return (y_prompt, y_sample)
```

```python
import functools

import jax
import jax.numpy as jnp
from jax import lax
from jax.experimental import pallas as pl
from jax.experimental.pallas import tpu as pltpu

D = 1024
DEPTH = 4
N_HEADS = 4
DK = 512
DV = 1024
HK = DK // N_HEADS
HV = DV // N_HEADS
RANK = 16
RANK_PAD = 128
TAU = 16.0
CHUNK = 64
SG_CHUNK = 128
SG_GROUPS = 4
SG_WIDTH = 1024
SG_GD = SG_WIDTH // SG_GROUPS
FFN_HIDDEN = 2816
FFN_CHUNK = 256
N_FFN_CHUNKS = FFN_HIDDEN // FFN_CHUNK
EPS = 1e-6

GLA_TILE = 256
SGU_TILE = 256
VMEM_LIMIT = 56 * 1024 * 1024

F32 = jnp.float32
BF16 = jnp.bfloat16


def _dot(a, b):
    return jnp.dot(a, b, preferred_element_type=F32)


def _dot_nt(a, b):
    return lax.dot_general(a, b, (((1,), (1,)), ((), ())), preferred_element_type=F32)


def _dot_tn(a, b):
    return lax.dot_general(a, b, (((0,), (0,)), ((), ())), preferred_element_type=F32)


def _sigmoid(x):
    return 1.0 / (1.0 + jnp.exp(-x))


def _rms(x, g):
    ms = jnp.mean(x * x, axis=-1, keepdims=True)
    return x * lax.rsqrt(ms + EPS) * g


def _log_sigmoid(x):
    return jnp.minimum(x, 0.0) - jnp.log1p(jnp.exp(-jnp.abs(x)))


def _split_bf16(x):
    hi = x.astype(BF16)
    lo = (x - hi.astype(F32)).astype(BF16)
    return hi, lo


def _const_spec(shape):
    zeros = (0,) * len(shape)
    return pl.BlockSpec(shape, lambda *_: zeros, pipeline_mode=pl.Buffered(1))


def _run(gen):
    for _ in gen:
        pass


def _interleave(*gens):
    live = list(gens)
    while live:
        for g in list(live):
            try:
                next(g)
            except StopIteration:
                live.remove(g)


def _ada_kernel(c_ref, w_ref, b_ref, o_ref):
    c = c_ref[...]
    s = (c * _sigmoid(c)).astype(BF16)
    o_ref[0] = _dot(s, w_ref[0].astype(BF16)) + b_ref[0]


def _ada_call(c_all, w_ada, b_ada):
    bp = c_all.shape[0]
    return pl.pallas_call(
        _ada_kernel,
        grid=(DEPTH, 6),
        in_specs=[
            pl.BlockSpec((bp, D), lambda i, j: (0, 0)),
            pl.BlockSpec((1, D, D), lambda i, j: (i, 0, j)),
            pl.BlockSpec((1, 1, D), lambda i, j: (i, 0, j)),
        ],
        out_specs=pl.BlockSpec((1, bp, D), lambda i, j: (i, 0, j)),
        out_shape=jax.ShapeDtypeStruct((DEPTH, bp, 6 * D), F32),
        name="ada_mod",
    )(c_all, w_ada, b_ada.reshape(DEPTH, 1, 6 * D))


def _ffn_steps(x1, mod_ref, ng_ref, wa_ref, wb_ref, wo_ref, out_fn):
    sh2 = mod_ref[0, :, 3 * D:4 * D]
    sc2 = mod_ref[0, :, 4 * D:5 * D]
    g2 = mod_ref[0, :, 5 * D:6 * D]
    hb = (_rms(x1, ng_ref[2:3, :]) * (1.0 + sc2) + sh2).astype(BF16)
    yield
    up = lambda c: (_dot(hb, wa_ref[c]), _dot(hb, wb_ref[c]))
    acc = None
    nxt = up(0)
    for c in range(N_FFN_CHUNKS):
        a, b = nxt
        if c + 1 < N_FFN_CHUNKS:
            nxt = up(c + 1)
        act = (a * _sigmoid(a) * b).astype(BF16)
        y = _dot(act, wo_ref[c])
        acc = y if acc is None else acc + y
        yield
    out_fn(x1 + g2 * _rms(acc, ng_ref[3:4, :]))


def _chunk_decay_columns(rows):
    pad = jnp.zeros((128 - len(rows), 128), F32)
    m = jnp.concatenate(rows + [pad], axis=0)
    return jnp.exp(m.T)


def _decay_operands(q, k, b, *, reverse):
    nc = q.shape[0] // CHUNK
    q_d = (q * jnp.exp(b)).astype(BF16)
    k_d = (k * jnp.exp(-b)).astype(BF16)
    k_e, rows = [], []
    for c in range(nc):
        r0 = c * CHUNK if reverse else c * CHUNK + CHUNK - 1
        tot = b[r0:r0 + 1, :]
        rs = slice(c * CHUNK, (c + 1) * CHUNK)
        k_e.append((k[rs, :] * jnp.exp(tot - b[rs, :])).astype(BF16))
        rows += [tot[:, h * HK:(h + 1) * HK] for h in range(N_HEADS)]
    return q_d, k_d, jnp.concatenate(k_e, axis=0), _chunk_decay_columns(rows)


def _scan_steps(qd_ref, kd_ref, ke_ref, v_ref, dec_ref, s_ref, out_fn, *, reverse):
    nc = qd_ref.shape[0] // CHUNK
    ri = lax.broadcasted_iota(jnp.int32, (CHUNK, CHUNK), 0)
    ci = lax.broadcasted_iota(jnp.int32, (CHUNK, CHUNK), 1)
    mask = (ci > ri) if reverse else (ci <= ri)
    dec = dec_ref[...]

    def independent(c):
        rs = slice(c * CHUNK, (c + 1) * CHUNK)
        res = []
        for h in range(N_HEADS):
            ks = slice(h * HK, (h + 1) * HK)
            vv = v_ref[rs, h * HV:(h + 1) * HV]
            res.append((_dot_nt(qd_ref[rs, ks], kd_ref[rs, ks]), _dot_tn(ke_ref[rs, ks], vv)))
        return res

    def dependent(c, res):
        rs = slice(c * CHUNK, (c + 1) * CHUNK)
        for h in range(N_HEADS):
            sc, u = res[h]
            vv = v_ref[rs, h * HV:(h + 1) * HV]
            s_old = s_ref[h]
            o = (_dot(jnp.where(mask, sc, 0.0).astype(BF16), vv)
                 + _dot(qd_ref[rs, h * HK:(h + 1) * HK], s_old.astype(BF16)))
            out_fn(c, h, o)
            j = c * N_HEADS + h
            s_ref[h] = s_old * dec[:, j:j + 1] + u

    order = list(range(nc - 1, -1, -1) if reverse else range(nc))
    prev = None
    for c in order:
        cur = (c, independent(c))
        if prev is not None:
            dependent(*prev)
        prev = cur
        yield
    dependent(*prev)
    yield


def _cumsum_matrix(tm, reverse):
    r = jnp.arange(tm)[:, None]
    c = jnp.arange(tm)[None, :]
    tri = (c >= r) if reverse else (c <= r)
    return ((r // CHUNK == c // CHUNK) & tri).astype(BF16)


def _gla_a_kernel(x_ref, mod_ref, ng_ref, t_ref, win_ref, wg1_ref, wg2_ref, bg_ref,
                  q_ref, k_ref, v_ref, r_ref, g1b_ref, of_ref,
                  s_ref, qd_scr, kd_scr, ke_scr, v_scr, dec_scr, *, per_seq):
    i = pl.program_id(0)
    wr = i % 2
    rd = (i + 1) % 2

    @pl.when(i == 0)
    def _():
        qd_scr[1] = jnp.zeros(qd_scr.shape[1:], BF16)
        kd_scr[1] = jnp.zeros(kd_scr.shape[1:], BF16)
        ke_scr[1] = jnp.zeros(ke_scr.shape[1:], BF16)
        v_scr[1] = jnp.zeros(v_scr.shape[1:], BF16)
        dec_scr[1] = jnp.zeros(dec_scr.shape[1:], F32)

    @pl.when((i == 0) | ((i - 1) % per_seq == 0))
    def _():
        s_ref[...] = jnp.zeros_like(s_ref)

    def proj():
        x = x_ref[...]
        sh1 = mod_ref[0, :, 0:D]
        sc1 = mod_ref[0, :, D:2 * D]
        hb = (_rms(x, ng_ref[0:1, :]) * (1.0 + sc1) + sh1).astype(BF16)
        yield
        g1 = _dot(hb, wg1_ref[...])
        g1b_ref[...] = g1[:, RANK_PAD:].astype(BF16)
        g1f = g1[:, :RANK_PAD].astype(BF16)
        for j in range(2):
            cs = slice(j * 512, (j + 1) * 512)
            vj = _dot(hb, win_ref[:, 2 * DK + j * 512:2 * DK + (j + 1) * 512]).astype(BF16)
            v_ref[:, cs] = vj
            v_scr[wr, :, cs] = vj
            if j == 0:
                gate = _dot(g1f, wg2_ref[0]) + bg_ref[0:1, :]
                la_hi, la_lo = _split_bf16(_log_sigmoid(gate) * (1.0 / TAU))
            yield
        t = t_ref[...]
        b = _dot(t, la_hi) + _dot(t, la_lo)
        r0 = 2 * DK + DV
        r_ref[:, 0:512] = _dot(hb, win_ref[:, r0:r0 + 512]).astype(BF16)
        yield
        q = _dot(hb, win_ref[:, 0:DK]) * (HK ** -0.5)
        k = _dot(hb, win_ref[:, DK:2 * DK])
        r_ref[:, 512:1024] = _dot(hb, win_ref[:, r0 + 512:r0 + 1024]).astype(BF16)
        yield
        q_ref[...] = q.astype(BF16)
        k_ref[...] = k.astype(BF16)
        q_d, k_d, k_e, dec = _decay_operands(q, k, b, reverse=False)
        qd_scr[wr] = q_d
        kd_scr[wr] = k_d
        ke_scr[wr] = k_e
        dec_scr[wr] = dec

    def put(c, h, o):
        of_ref[c * CHUNK:(c + 1) * CHUNK, h * HV:(h + 1) * HV] = o.astype(BF16)

    scan = _scan_steps(qd_scr.at[rd], kd_scr.at[rd], ke_scr.at[rd], v_scr.at[rd], dec_scr.at[rd],
                       s_ref, put, reverse=False)
    _interleave(scan, proj())


def _gla_a_call(x, mod_l, ng_l, w_in, wg1, wg2, bg):
    B, L, _ = x.shape
    tm = GLA_TILE
    n = (B * L) // tm
    per_seq = L // tm
    cur = lambda i: jnp.minimum(i, n - 1)
    prev = lambda i: jnp.maximum(i - 1, 0)
    tok = lambda w, f: pl.BlockSpec((tm, w), lambda i: (f(i), 0))
    out_shapes = (
        jax.ShapeDtypeStruct((B * L, DK), BF16),
        jax.ShapeDtypeStruct((B * L, DK), BF16),
        jax.ShapeDtypeStruct((B * L, DV), BF16),
        jax.ShapeDtypeStruct((B * L, DV), BF16),
        jax.ShapeDtypeStruct((B * L, RANK_PAD), BF16),
        jax.ShapeDtypeStruct((B * L, DV), BF16),
    )
    return pl.pallas_call(
        functools.partial(_gla_a_kernel, per_seq=per_seq),
        grid=(n + 1,),
        in_specs=[
            tok(D, cur),
            pl.BlockSpec((1, 1, 6 * D), lambda i: (cur(i) // per_seq, 0, 0)),
            _const_spec((4, D)),
            _const_spec((tm, tm)),
            _const_spec((D, 2 * DK + 2 * DV)),
            _const_spec((D, 2 * RANK_PAD)),
            _const_spec((2, RANK_PAD, DK)),
            _const_spec((2, DK)),
        ],
        out_specs=(tok(DK, cur), tok(DK, cur), tok(DV, cur), tok(DV, cur), tok(RANK_PAD, cur),
                   tok(DV, prev)),
        out_shape=out_shapes,
        scratch_shapes=[
            pltpu.VMEM((N_HEADS, HK, HV), F32),
            pltpu.VMEM((2, tm, DK), BF16), pltpu.VMEM((2, tm, DK), BF16),
            pltpu.VMEM((2, tm, DK), BF16), pltpu.VMEM((2, tm, DV), BF16),
            pltpu.VMEM((2, 128, 128), F32),
        ],
        compiler_params=pltpu.CompilerParams(
            dimension_semantics=("arbitrary",), vmem_limit_bytes=VMEM_LIMIT),
        name="gla_a",
    )(x.reshape(B * L, D), mod_l, ng_l, _cumsum_matrix(tm, False), w_in, wg1, wg2, bg)


def _gla_b_kernel(x_ref, modm_ref, modf_ref, ng_ref, t_ref, q_ref, k_ref, v_ref, r_ref, g1b_ref,
                  of_ref, wg2_ref, bg_ref, gh_ref, wout_ref, wa_ref, wb_ref, wo_ref,
                  y_ref, s_ref, o_scr, m_scr, qd_scr, kd_scr, ke_scr, dec_scr, x1_scr, *, per_seq):
    i = pl.program_id(0)

    @pl.when(i == 0)
    def _():
        x1_scr[1] = jnp.zeros(x1_scr.shape[1:], F32)

    @pl.when(i % per_seq == 0)
    def _():
        s_ref[...] = jnp.zeros_like(s_ref)

    def mixer():
        gate = _dot(g1b_ref[...], wg2_ref[1]) + bg_ref[1:2, :]
        la_hi, la_lo = _split_bf16(_log_sigmoid(gate) * (1.0 / TAU))
        yield
        t = t_ref[...]
        b = _dot(t, la_hi) + _dot(t, la_lo)
        q_d, k_d, k_e, dec = _decay_operands(q_ref[...].astype(F32), k_ref[...].astype(F32), b,
                                             reverse=True)
        qd_scr[...] = q_d
        kd_scr[...] = k_d
        ke_scr[...] = k_e
        dec_scr[...] = dec
        yield

        def put(c, h, o):
            rs = slice(c * CHUNK, (c + 1) * CHUNK)
            vs = slice(h * HV, (h + 1) * HV)
            o_scr[rs, vs] = o + of_ref[rs, vs].astype(F32)

        yield from _scan_steps(qd_scr, kd_scr, ke_scr, v_ref, dec_scr, s_ref, put, reverse=True)
        gh = gh_ref[...]
        for h in range(N_HEADS):
            vs = slice(h * HV, (h + 1) * HV)
            rh = r_ref[:, vs].astype(F32)
            m_scr[:, vs] = (_rms(o_scr[:, vs], gh) * (rh * _sigmoid(rh))).astype(BF16)
            if h % 2 == 1:
                yield
        ys = []
        for j in range(2):
            ys.append(_dot(m_scr[...], wout_ref[:, j * 512:(j + 1) * 512]))
            yield
        y = jnp.concatenate(ys, axis=-1)
        g1 = modm_ref[0, :, 2 * D:3 * D]
        x1_scr[i % 2] = x_ref[...] + g1 * _rms(y, ng_ref[1:2, :])

    def put_y(val):
        y_ref[...] = val

    ffn = _ffn_steps(x1_scr[(i + 1) % 2], modf_ref, ng_ref, wa_ref, wb_ref, wo_ref, put_y)
    _interleave(ffn, mixer())


def _gla_b_call(x, mod_l, ng_l, q, k, v, r, g1b, o_f, wg2, bg, g_head, w_out, wa, wb, wo):
    B, L, _ = x.shape
    tm = GLA_TILE
    n = (B * L) // tm
    per_seq = L // tm
    tile = lambda j: (j // per_seq) * per_seq + (per_seq - 1 - j % per_seq)
    cur = lambda i: jnp.minimum(i, n - 1)
    prev = lambda i: jnp.maximum(i - 1, 0)
    tok = lambda w: pl.BlockSpec((tm, w), lambda i: (tile(cur(i)), 0))
    out = pl.pallas_call(
        functools.partial(_gla_b_kernel, per_seq=per_seq),
        grid=(n + 1,),
        in_specs=[
            tok(D),
            pl.BlockSpec((1, 1, 6 * D), lambda i: (cur(i) // per_seq, 0, 0)),
            pl.BlockSpec((1, 1, 6 * D), lambda i: (prev(i) // per_seq, 0, 0)),
            _const_spec((4, D)),
            _const_spec((tm, tm)),
            tok(DK), tok(DK), tok(DV), tok(DV), tok(RANK_PAD), tok(DV),
            _const_spec((2, RANK_PAD, DK)),
            _const_spec((2, DK)),
            _const_spec((1, HV)),
            _const_spec((DV, D)),
            _const_spec((N_FFN_CHUNKS, D, FFN_CHUNK)),
            _const_spec((N_FFN_CHUNKS, D, FFN_CHUNK)),
            _const_spec((N_FFN_CHUNKS, FFN_CHUNK, D)),
        ],
        out_specs=pl.BlockSpec((tm, D), lambda i: (tile(prev(i)), 0)),
        out_shape=jax.ShapeDtypeStruct((B * L, D), F32),
        scratch_shapes=[
            pltpu.VMEM((N_HEADS, HK, HV), F32),
            pltpu.VMEM((tm, DV), F32),
            pltpu.VMEM((tm, DV), BF16),
            pltpu.VMEM((tm, DK), BF16), pltpu.VMEM((tm, DK), BF16), pltpu.VMEM((tm, DK), BF16),
            pltpu.VMEM((128, 128), F32),
            pltpu.VMEM((2, tm, D), F32),
        ],
        compiler_params=pltpu.CompilerParams(
            dimension_semantics=("arbitrary",), vmem_limit_bytes=VMEM_LIMIT),
        name="gla_b",
    )(x.reshape(B * L, D), mod_l, mod_l, ng_l, _cumsum_matrix(tm, True), q, k, v, r, g1b, o_f,
      wg2, bg, g_head, w_out, wa, wb, wo)
    return out.reshape(B, L, D)


def _gelu_tanh(x):
    c = 0.7978845608028654
    return 0.5 * x * (1.0 + jnp.tanh(c * (x + 0.044715 * (x * x * x))))


def _sgu_kernel(x_ref, modm_ref, modf_ref, ng_ref, win_ref, bin_ref, lng_ref, lnb_ref, ws_ref,
                bs_ref, wout_ref, wa_ref, wb_ref, wo_ref, y_ref, m_scr, x1_scr):
    i = pl.program_id(0)
    tm = x_ref.shape[0]

    @pl.when(i == 0)
    def _():
        x1_scr[1] = jnp.zeros((tm, D), F32)

    def mixer():
        sh1 = modm_ref[0, :, 0:D]
        sc1 = modm_ref[0, :, D:2 * D]
        g1 = modm_ref[0, :, 2 * D:3 * D]
        x = x_ref[...]
        hb = (_rms(x, ng_ref[0:1, :]) * (1.0 + sc1) + sh1).astype(BF16)
        yield
        zs = []
        for j in range(4):
            cs = slice(j * 512, (j + 1) * 512)
            zs.append(_gelu_tanh(_dot(hb, win_ref[:, cs]) + bin_ref[:, cs]))
            yield
        v = jnp.concatenate(zs[2:], axis=-1)
        mu = jnp.mean(v, axis=-1, keepdims=True)
        vc = v - mu
        var = jnp.mean(vc * vc, axis=-1, keepdims=True)
        vn = (vc * lax.rsqrt(var + EPS) * lng_ref[...] + lnb_ref[...]).astype(BF16)
        yield
        for n in range(tm // SG_CHUNK):
            rs = slice(n * SG_CHUNK, (n + 1) * SG_CHUNK)
            for g in range(SG_GROUPS):
                cs = slice(g * SG_GD, (g + 1) * SG_GD)
                u = zs[g // 2][rs, (g % 2) * SG_GD:(g % 2 + 1) * SG_GD]
                mixed = _dot(ws_ref[g], vn[rs, cs]) + bs_ref[:, cs]
                m_scr[rs, cs] = (u * mixed).astype(BF16)
            yield
        ys = []
        for j in range(2):
            ys.append(_dot(m_scr[...], wout_ref[:, j * 512:(j + 1) * 512]))
            yield
        y = jnp.concatenate(ys, axis=-1)
        x1_scr[i % 2] = x + g1 * _rms(y, ng_ref[1:2, :])

    def put(val):
        y_ref[...] = val

    ffn = _ffn_steps(x1_scr[(i + 1) % 2], modf_ref, ng_ref, wa_ref, wb_ref, wo_ref, put)
    _interleave(ffn, mixer())


def _sgu_call(x, mod_l, ng_l, w_in, b_in, ln_g, ln_b, w_s, bs_full, w_out, wa, wb, wo):
    B, L, _ = x.shape
    tm = SGU_TILE
    n = (B * L) // tm
    per_seq = L // tm
    cur = lambda i: jnp.minimum(i, n - 1)
    prev = lambda i: jnp.maximum(i - 1, 0)
    out = pl.pallas_call(
        _sgu_kernel,
        grid=(n + 1,),
        in_specs=[
            pl.BlockSpec((tm, D), lambda i: (cur(i), 0)),
            pl.BlockSpec((1, 1, 6 * D), lambda i: (cur(i) // per_seq, 0, 0)),
            pl.BlockSpec((1, 1, 6 * D), lambda i: (prev(i) // per_seq, 0, 0)),
            _const_spec((4, D)),
            _const_spec((D, 2 * SG_WIDTH)),
            _const_spec((1, 2 * SG_WIDTH)),
            _const_spec((1, SG_WIDTH)),
            _const_spec((1, SG_WIDTH)),
            _const_spec((SG_GROUPS, SG_CHUNK, SG_CHUNK)),
            _const_spec((SG_CHUNK, SG_WIDTH)),
            _const_spec((SG_WIDTH, D)),
            _const_spec((N_FFN_CHUNKS, D, FFN_CHUNK)),
            _const_spec((N_FFN_CHUNKS, D, FFN_CHUNK)),
            _const_spec((N_FFN_CHUNKS, FFN_CHUNK, D)),
        ],
        out_specs=pl.BlockSpec((tm, D), lambda i: (prev(i), 0)),
        out_shape=jax.ShapeDtypeStruct((B * L, D), F32),
        scratch_shapes=[pltpu.VMEM((tm, SG_WIDTH), BF16), pltpu.VMEM((2, tm, D), F32)],
        compiler_params=pltpu.CompilerParams(
            dimension_semantics=("arbitrary",), vmem_limit_bytes=VMEM_LIMIT),
        name="sgu_layer",
    )(x.reshape(B * L, D), mod_l, mod_l, ng_l, w_in, b_in, ln_g, ln_b, w_s, bs_full, w_out,
      wa, wb, wo)
    return out.reshape(B, L, D)


def _prep_ffn(ffn_w_in, ffn_w_out, i):
    w = ffn_w_in[i].astype(BF16)
    wa = w[:, :FFN_HIDDEN].reshape(D, N_FFN_CHUNKS, FFN_CHUNK).transpose(1, 0, 2)
    wb = w[:, FFN_HIDDEN:].reshape(D, N_FFN_CHUNKS, FFN_CHUNK).transpose(1, 0, 2)
    wo = ffn_w_out[i].astype(BF16).reshape(N_FFN_CHUNKS, FFN_CHUNK, D)
    return wa, wb, wo


def _prep_gate(gla_w_gk1, gla_w_gk2, j):
    wg1 = jnp.zeros((D, 2 * RANK_PAD), BF16)
    wg1 = wg1.at[:, :RANK].set(gla_w_gk1[j, 0].astype(BF16))
    wg1 = wg1.at[:, RANK_PAD:RANK_PAD + RANK].set(gla_w_gk1[j, 1].astype(BF16))
    wg2 = jnp.zeros((2, RANK_PAD, DK), BF16).at[:, :RANK, :].set(gla_w_gk2[j].astype(BF16))
    return wg1, wg2


def _trunk(x, mod, norm_g, gla_w_in, gla_w_gk1, gla_w_gk2, gla_b_gk, gla_g_head, gla_w_out,
           sg_w_in, sg_b_in, sg_ln_g, sg_ln_b, sg_w_s, sg_b_s, sg_w_out, ffn):
    B = x.shape[0]
    for i in range(DEPTH):
        mod_l = mod[i].reshape(B, 1, 6 * D)
        wa, wb, wo = ffn[i]
        j = i // 2
        if i % 2 == 0:
            wg1, wg2 = _prep_gate(gla_w_gk1, gla_w_gk2, j)
            q, k, v, r, g1b, o_f = _gla_a_call(
                x, mod_l, norm_g[i], gla_w_in[j].astype(BF16), wg1, wg2, gla_b_gk[j])
            x = _gla_b_call(x, mod_l, norm_g[i], q, k, v, r, g1b, o_f, wg2, gla_b_gk[j],
                            gla_g_head[j].reshape(1, HV), gla_w_out[j].astype(BF16), wa, wb, wo)
        else:
            bs_full = jnp.repeat(sg_b_s[j].T, SG_GD, axis=1)
            x = _sgu_call(x, mod_l, norm_g[i], sg_w_in[j].astype(BF16),
                          sg_b_in[j].reshape(1, -1), sg_ln_g[j].reshape(1, -1),
                          sg_ln_b[j].reshape(1, -1), sg_w_s[j].astype(BF16), bs_full,
                          sg_w_out[j].astype(BF16), wa, wb, wo)
    return x


def kernel(x_prompt, x_sample, c_prompt, c_sample, norm_g, w_ada, b_ada, gla_w_in, gla_w_gk1,
           gla_w_gk2, gla_b_gk, gla_g_head, gla_w_out, sg_w_in, sg_b_in, sg_ln_g, sg_ln_b,
           sg_w_s, sg_b_s, sg_w_out, ffn_w_in, ffn_w_out):
    bp, bs = c_prompt.shape[0], c_sample.shape[0]
    rows = -(-(bp + bs) // 16) * 16
    c_all = jnp.concatenate([c_prompt, c_sample, jnp.zeros((rows - bp - bs, D), F32)], axis=0)
    mod = _ada_call(c_all, w_ada, b_ada)
    ffn = [_prep_ffn(ffn_w_in, ffn_w_out, i) for i in range(DEPTH)]
    args = (norm_g, gla_w_in, gla_w_gk1, gla_w_gk2, gla_b_gk, gla_g_head, gla_w_out,
            sg_w_in, sg_b_in, sg_ln_g, sg_ln_b, sg_w_s, sg_b_s, sg_w_out, ffn)
    y_prompt = _trunk(x_prompt, mod[:, :bp], *args)
    y_sample = _trunk(x_sample, mod[:, bp:bp + bs], *args)
    return (y_prompt, y_sample)
```

```python
import functools

import jax
import jax.numpy as jnp
from jax import lax
from jax.experimental import pallas as pl
from jax.experimental.pallas import tpu as pltpu

D = 1024
DEPTH = 4
N_HEADS = 4
DK = 512
DV = 1024
HK = DK // N_HEADS
HV = DV // N_HEADS
RANK = 16
RANK_PAD = 128
TAU = 16.0
CHUNK = 64
SG_CHUNK = 128
SG_GROUPS = 4
SG_WIDTH = 1024
SG_GD = SG_WIDTH // SG_GROUPS
FFN_HIDDEN = 2816
FFN_CHUNK = 256
N_FFN_CHUNKS = FFN_HIDDEN // FFN_CHUNK
EPS = 1e-6

GLA_TILE = 512
SGU_TILE = 512
VMEM_LIMIT = 56 * 1024 * 1024

F32 = jnp.float32
BF16 = jnp.bfloat16


def _dot(a, b):
    return jnp.dot(a, b, preferred_element_type=F32)


def _dot_nt(a, b):
    return lax.dot_general(a, b, (((1,), (1,)), ((), ())), preferred_element_type=F32)


def _dot_tn(a, b):
    return lax.dot_general(a, b, (((0,), (0,)), ((), ())), preferred_element_type=F32)


def _sigmoid(x):
    return 1.0 / (1.0 + jnp.exp(-x))


def _rms(x, g):
    ms = jnp.mean(x * x, axis=-1, keepdims=True)
    return x * lax.rsqrt(ms + EPS) * g


def _log_sigmoid(x):
    return jnp.minimum(x, 0.0) - jnp.log1p(jnp.exp(-jnp.abs(x)))


def _split_bf16(x):
    hi = x.astype(BF16)
    lo = (x - hi.astype(F32)).astype(BF16)
    return hi, lo


def _const_spec(shape):
    zeros = (0,) * len(shape)
    return pl.BlockSpec(shape, lambda *_: zeros, pipeline_mode=pl.Buffered(1))


def _row_halves(n):
    return [slice(0, n // 2), slice(n // 2, n)]


def _interleave(*gens, pattern=()):
    live = list(gens)

    def advance(g):
        try:
            next(g)
        except StopIteration:
            live.remove(g)

    for idx in pattern:
        if gens[idx] in live:
            advance(gens[idx])
    while live:
        for g in list(live):
            advance(g)


def _ada_kernel(c_ref, w_ref, b_ref, o_ref):
    c = c_ref[...]
    s = (c * _sigmoid(c)).astype(BF16)
    o_ref[0] = _dot(s, w_ref[0].astype(BF16)) + b_ref[0]


def _ada_call(c_all, w_ada, b_ada):
    bp = c_all.shape[0]
    return pl.pallas_call(
        _ada_kernel,
        grid=(DEPTH, 6),
        in_specs=[
            pl.BlockSpec((bp, D), lambda i, j: (0, 0)),
            pl.BlockSpec((1, D, D), lambda i, j: (i, 0, j)),
            pl.BlockSpec((1, 1, D), lambda i, j: (i, 0, j)),
        ],
        out_specs=pl.BlockSpec((1, bp, D), lambda i, j: (i, 0, j)),
        out_shape=jax.ShapeDtypeStruct((DEPTH, bp, 6 * D), F32),
        name="ada_mod",
    )(c_all, w_ada, b_ada.reshape(DEPTH, 1, 6 * D))


def _adaln_steps(x_ref, g, sc, sh, out):
    gs = g * (1.0 + sc)
    for rs in _row_halves(x_ref.shape[0]):
        out.append((_rms(x_ref[rs, :], gs) + sh).astype(BF16))
        yield


def _residual_steps(x_ref, y, gain, g, dst_ref):
    gg = gain * g
    for rs in _row_halves(x_ref.shape[0]):
        dst_ref[rs, :] = x_ref[rs, :] + _rms(y[rs, :], gg)
        yield


def _ffn_steps(x1_ref, mod_ref, ng_ref, wa_ref, wb_ref, wo_ref, act_scr, y_ref):
    sh2 = mod_ref[0, :, 3 * D:4 * D]
    sc2 = mod_ref[0, :, 4 * D:5 * D]
    g2 = mod_ref[0, :, 5 * D:6 * D]
    hbs = []
    yield from _adaln_steps(x1_ref, ng_ref[2:3, :], sc2, sh2, hbs)
    hb = jnp.concatenate(hbs, axis=0)
    for c in range(N_FFN_CHUNKS):
        a = _dot(hb, wa_ref[c])
        b = _dot(hb, wb_ref[c])
        act_scr[:, c * FFN_CHUNK:(c + 1) * FFN_CHUNK] = (a * _sigmoid(a) * b).astype(BF16)
        yield
    ys = []
    for j in range(2):
        ys.append(_dot(act_scr[...], wo_ref[:, j * 512:(j + 1) * 512]))
        yield
    yield from _residual_steps(x1_ref, jnp.concatenate(ys, axis=-1), g2, ng_ref[3:4, :], y_ref)


def _chunk_decay_columns(rows):
    pad = jnp.zeros((128 - len(rows), 128), F32)
    m = jnp.concatenate(rows + [pad], axis=0)
    return jnp.exp(m.T)


def _decay_qk(q, k, b):
    return (q * jnp.exp(b)).astype(BF16), (k * jnp.exp(-b)).astype(BF16)


def _decay_chunk_end(k, b, *, reverse):
    nc = k.shape[0] // CHUNK
    k_e, rows = [], []
    for c in range(nc):
        r0 = c * CHUNK if reverse else c * CHUNK + CHUNK - 1
        tot = b[r0:r0 + 1, :]
        rs = slice(c * CHUNK, (c + 1) * CHUNK)
        k_e.append((k[rs, :] * jnp.exp(tot - b[rs, :])).astype(BF16))
        rows += [tot[:, h * HK:(h + 1) * HK] for h in range(N_HEADS)]
    return jnp.concatenate(k_e, axis=0), _chunk_decay_columns(rows)


def _scan_steps(qd_ref, kd_ref, ke_ref, v_ref, dec_ref, s_ref, out_fn, *, reverse):
    nc = qd_ref.shape[0] // CHUNK
    ri = lax.broadcasted_iota(jnp.int32, (CHUNK, CHUNK), 0)
    ci = lax.broadcasted_iota(jnp.int32, (CHUNK, CHUNK), 1)
    mask = (ci > ri) if reverse else (ci <= ri)
    dec = dec_ref[...]

    def independent(c):
        rs = slice(c * CHUNK, (c + 1) * CHUNK)
        res = []
        for h in range(N_HEADS):
            ks = slice(h * HK, (h + 1) * HK)
            vv = v_ref[rs, h * HV:(h + 1) * HV]
            res.append((_dot_nt(qd_ref[rs, ks], kd_ref[rs, ks]), _dot_tn(ke_ref[rs, ks], vv)))
        return res

    def dependent(c, res):
        rs = slice(c * CHUNK, (c + 1) * CHUNK)
        for h in range(N_HEADS):
            sc, u = res[h]
            vv = v_ref[rs, h * HV:(h + 1) * HV]
            s_old = s_ref[h]
            o = (_dot(jnp.where(mask, sc, 0.0).astype(BF16), vv)
                 + _dot(qd_ref[rs, h * HK:(h + 1) * HK], s_old.astype(BF16)))
            out_fn(c, h, o)
            j = c * N_HEADS + h
            s_ref[h] = s_old * dec[:, j:j + 1] + u

    order = list(range(nc - 1, -1, -1) if reverse else range(nc))
    prev = None
    for c in order:
        cur = (c, independent(c))
        if prev is not None:
            dependent(*prev)
        prev = cur
        yield
    dependent(*prev)
    yield


CUMSUM_BLOCK = 256


def _cumsum_matrix(reverse):
    r = jnp.arange(CUMSUM_BLOCK)[:, None]
    c = jnp.arange(CUMSUM_BLOCK)[None, :]
    tri = (c >= r) if reverse else (c <= r)
    return ((r // CHUNK == c // CHUNK) & tri).astype(BF16)


def _chunk_cumsum(t, hi, lo):
    blocks = []
    for r0 in range(0, hi.shape[0], CUMSUM_BLOCK):
        rs = slice(r0, r0 + CUMSUM_BLOCK)
        blocks.append(_dot(t, hi[rs, :]) + _dot(t, lo[rs, :]))
    return jnp.concatenate(blocks, axis=0)


def _gla_a_kernel(x_ref, mod_ref, ng_ref, t_ref, win_ref, wg1_ref, wg2_ref, bg_ref,
                  q_ref, k_ref, v_ref, r_ref, g1b_ref, of_ref,
                  s_ref, qd_scr, kd_scr, ke_scr, v_scr, dec_scr, *, per_seq):
    i = pl.program_id(0)
    wr = i % 2
    rd = (i + 1) % 2

    @pl.when(i == 0)
    def _():
        qd_scr[1] = jnp.zeros(qd_scr.shape[1:], BF16)
        kd_scr[1] = jnp.zeros(kd_scr.shape[1:], BF16)
        ke_scr[1] = jnp.zeros(ke_scr.shape[1:], BF16)
        v_scr[1] = jnp.zeros(v_scr.shape[1:], BF16)
        dec_scr[1] = jnp.zeros(dec_scr.shape[1:], F32)

    @pl.when((i == 0) | ((i - 1) % per_seq == 0))
    def _():
        s_ref[...] = jnp.zeros_like(s_ref)

    def proj():
        hbs = []
        yield from _adaln_steps(x_ref, ng_ref[0:1, :], mod_ref[0, :, D:2 * D], mod_ref[0, :, 0:D],
                                hbs)
        hb = jnp.concatenate(hbs, axis=0)
        g1 = _dot(hb, wg1_ref[...])
        g1b_ref[...] = g1[:, RANK_PAD:].astype(BF16)
        g1f = g1[:, :RANK_PAD].astype(BF16)
        for j in range(2):
            cs = slice(j * 512, (j + 1) * 512)
            vj = _dot(hb, win_ref[:, 2 * DK + j * 512:2 * DK + (j + 1) * 512]).astype(BF16)
            v_ref[:, cs] = vj
            v_scr[wr, :, cs] = vj
            if j == 0:
                gate = _dot(g1f, wg2_ref[0]) + bg_ref[0:1, :]
                la_hi, la_lo = _split_bf16(_log_sigmoid(gate) * (1.0 / TAU))
            yield
        b = _chunk_cumsum(t_ref[...], la_hi, la_lo)
        r0 = 2 * DK + DV
        r_ref[:, 0:512] = _dot(hb, win_ref[:, r0:r0 + 512]).astype(BF16)
        yield
        q = _dot(hb, win_ref[:, 0:DK]) * (HK ** -0.5)
        k = _dot(hb, win_ref[:, DK:2 * DK])
        r_ref[:, 512:1024] = _dot(hb, win_ref[:, r0 + 512:r0 + 1024]).astype(BF16)
        yield
        q_ref[...] = q.astype(BF16)
        k_ref[...] = k.astype(BF16)
        qd_scr[wr], kd_scr[wr] = _decay_qk(q, k, b)
        ke_scr[wr], dec_scr[wr] = _decay_chunk_end(k, b, reverse=False)

    def put(c, h, o):
        of_ref[c * CHUNK:(c + 1) * CHUNK, h * HV:(h + 1) * HV] = o.astype(BF16)

    scan = _scan_steps(qd_scr.at[rd], kd_scr.at[rd], ke_scr.at[rd], v_scr.at[rd], dec_scr.at[rd],
                       s_ref, put, reverse=False)
    _interleave(scan, proj())


def _gla_a_call(x, mod_l, ng_l, w_in, wg1, wg2, bg):
    B, L, _ = x.shape
    tm = GLA_TILE
    n = (B * L) // tm
    per_seq = L // tm
    cur = lambda i: jnp.minimum(i, n - 1)
    prev = lambda i: jnp.maximum(i - 1, 0)
    tok = lambda w, f: pl.BlockSpec((tm, w), lambda i: (f(i), 0))
    out_shapes = (
        jax.ShapeDtypeStruct((B * L, DK), BF16),
        jax.ShapeDtypeStruct((B * L, DK), BF16),
        jax.ShapeDtypeStruct((B * L, DV), BF16),
        jax.ShapeDtypeStruct((B * L, DV), BF16),
        jax.ShapeDtypeStruct((B * L, RANK_PAD), BF16),
        jax.ShapeDtypeStruct((B * L, DV), BF16),
    )
    return pl.pallas_call(
        functools.partial(_gla_a_kernel, per_seq=per_seq),
        grid=(n + 1,),
        in_specs=[
            tok(D, cur),
            pl.BlockSpec((1, 1, 6 * D), lambda i: (cur(i) // per_seq, 0, 0)),
            _const_spec((4, D)),
            _const_spec((CUMSUM_BLOCK, CUMSUM_BLOCK)),
            _const_spec((D, 2 * DK + 2 * DV)),
            _const_spec((D, 2 * RANK_PAD)),
            _const_spec((2, RANK_PAD, DK)),
            _const_spec((2, DK)),
        ],
        out_specs=(tok(DK, cur), tok(DK, cur), tok(DV, cur), tok(DV, cur), tok(RANK_PAD, cur),
                   tok(DV, prev)),
        out_shape=out_shapes,
        scratch_shapes=[
            pltpu.VMEM((N_HEADS, HK, HV), F32),
            pltpu.VMEM((2, tm, DK), BF16), pltpu.VMEM((2, tm, DK), BF16),
            pltpu.VMEM((2, tm, DK), BF16), pltpu.VMEM((2, tm, DV), BF16),
            pltpu.VMEM((2, 128, 128), F32),
        ],
        compiler_params=pltpu.CompilerParams(
            dimension_semantics=("arbitrary",), vmem_limit_bytes=VMEM_LIMIT),
        name="gla_a",
    )(x.reshape(B * L, D), mod_l, ng_l, _cumsum_matrix(False), w_in, wg1, wg2, bg)


def _gla_b_kernel(x_ref, modm_ref, modf_ref, ng_ref, t_ref, q_ref, k_ref, v_ref, r_ref, g1b_ref,
                  of_ref, wg2_ref, bg_ref, gh_ref, wout_ref, wa_ref, wb_ref, wo_ref,
                  y_ref, s_ref, o_scr, m_scr, qd_scr, kd_scr, ke_scr, dec_scr, x1_scr, act_scr,
                  *, per_seq):
    i = pl.program_id(0)

    @pl.when(i == 0)
    def _():
        x1_scr[1] = jnp.zeros(x1_scr.shape[1:], F32)

    @pl.when(i % per_seq == 0)
    def _():
        s_ref[...] = jnp.zeros_like(s_ref)

    def mixer():
        gate = _dot(g1b_ref[...], wg2_ref[1]) + bg_ref[1:2, :]
        la_hi, la_lo = _split_bf16(_log_sigmoid(gate) * (1.0 / TAU))
        yield
        b = _chunk_cumsum(t_ref[...], la_hi, la_lo)
        k = k_ref[...].astype(F32)
        qd_scr[...], kd_scr[...] = _decay_qk(q_ref[...].astype(F32), k, b)
        yield
        ke_scr[...], dec_scr[...] = _decay_chunk_end(k, b, reverse=True)
        yield

        def put(c, h, o):
            rs = slice(c * CHUNK, (c + 1) * CHUNK)
            vs = slice(h * HV, (h + 1) * HV)
            o_scr[rs, vs] = o + of_ref[rs, vs].astype(F32)

        yield from _scan_steps(qd_scr, kd_scr, ke_scr, v_ref, dec_scr, s_ref, put, reverse=True)
        gh = gh_ref[...]
        for h in range(N_HEADS):
            vs = slice(h * HV, (h + 1) * HV)
            rh = r_ref[:, vs].astype(F32)
            m_scr[:, vs] = (_rms(o_scr[:, vs], gh) * (rh * _sigmoid(rh))).astype(BF16)
            yield
        ys = []
        for j in range(2):
            ys.append(_dot(m_scr[...], wout_ref[:, j * 512:(j + 1) * 512]))
            yield
        yield from _residual_steps(x_ref, jnp.concatenate(ys, axis=-1), modm_ref[0, :, 2 * D:3 * D],
                                   ng_ref[1:2, :], x1_scr.at[i % 2])

    ffn = _ffn_steps(x1_scr.at[(i + 1) % 2], modf_ref, ng_ref, wa_ref, wb_ref, wo_ref, act_scr, y_ref)
    _interleave(ffn, mixer(), pattern=(0, 0))


def _gla_b_call(x, mod_l, ng_l, q, k, v, r, g1b, o_f, wg2, bg, g_head, w_out, wa, wb, wo):
    B, L, _ = x.shape
    tm = GLA_TILE
    n = (B * L) // tm
    per_seq = L // tm
    tile = lambda j: (j // per_seq) * per_seq + (per_seq - 1 - j % per_seq)
    cur = lambda i: jnp.minimum(i, n - 1)
    prev = lambda i: jnp.maximum(i - 1, 0)
    tok = lambda w: pl.BlockSpec((tm, w), lambda i: (tile(cur(i)), 0))
    out = pl.pallas_call(
        functools.partial(_gla_b_kernel, per_seq=per_seq),
        grid=(n + 1,),
        in_specs=[
            tok(D),
            pl.BlockSpec((1, 1, 6 * D), lambda i: (cur(i) // per_seq, 0, 0)),
            pl.BlockSpec((1, 1, 6 * D), lambda i: (prev(i) // per_seq, 0, 0)),
            _const_spec((4, D)),
            _const_spec((CUMSUM_BLOCK, CUMSUM_BLOCK)),
            tok(DK), tok(DK), tok(DV), tok(DV), tok(RANK_PAD), tok(DV),
            _const_spec((2, RANK_PAD, DK)),
            _const_spec((2, DK)),
            _const_spec((1, HV)),
            _const_spec((DV, D)),
            _const_spec((N_FFN_CHUNKS, D, FFN_CHUNK)),
            _const_spec((N_FFN_CHUNKS, D, FFN_CHUNK)),
            _const_spec((FFN_HIDDEN, D)),
        ],
        out_specs=pl.BlockSpec((tm, D), lambda i: (tile(prev(i)), 0)),
        out_shape=jax.ShapeDtypeStruct((B * L, D), F32),
        scratch_shapes=[
            pltpu.VMEM((N_HEADS, HK, HV), F32),
            pltpu.VMEM((tm, DV), F32),
            pltpu.VMEM((tm, DV), BF16),
            pltpu.VMEM((tm, DK), BF16), pltpu.VMEM((tm, DK), BF16), pltpu.VMEM((tm, DK), BF16),
            pltpu.VMEM((128, 128), F32),
            pltpu.VMEM((2, tm, D), F32),
            pltpu.VMEM((tm, FFN_HIDDEN), BF16),
        ],
        compiler_params=pltpu.CompilerParams(
            dimension_semantics=("arbitrary",), vmem_limit_bytes=VMEM_LIMIT),
        name="gla_b",
    )(x.reshape(B * L, D), mod_l, mod_l, ng_l, _cumsum_matrix(True), q, k, v, r, g1b, o_f,
      wg2, bg, g_head, w_out, wa, wb, wo)
    return out.reshape(B, L, D)


GELU_C = 0.7978845608028654


def _gelu_tanh(x):
    return x * (0.5 + 0.5 * jnp.tanh(x * (GELU_C + (GELU_C * 0.044715) * (x * x))))


def _sgu_kernel(x_ref, modm_ref, modf_ref, ng_ref, win_ref, bin_ref, lng_ref, lnb_ref, ws_ref,
                bs_ref, wout_ref, wa_ref, wb_ref, wo_ref, y_ref, m_scr, x1_scr, act_scr):
    i = pl.program_id(0)
    tm = x_ref.shape[0]

    @pl.when(i == 0)
    def _():
        x1_scr[1] = jnp.zeros((tm, D), F32)

    def mixer():
        hbs = []
        yield from _adaln_steps(x_ref, ng_ref[0:1, :], modm_ref[0, :, D:2 * D],
                                modm_ref[0, :, 0:D], hbs)
        hb = jnp.concatenate(hbs, axis=0)
        zs = []
        for j in range(4):
            cs = slice(j * 512, (j + 1) * 512)
            zs.append(_gelu_tanh(_dot(hb, win_ref[:, cs]) + bin_ref[:, cs]))
            yield
        vns = []
        for rs in _row_halves(tm):
            v = jnp.concatenate([zs[2][rs, :], zs[3][rs, :]], axis=-1)
            mu = jnp.mean(v, axis=-1, keepdims=True)
            vc = v - mu
            var = jnp.mean(vc * vc, axis=-1, keepdims=True)
            vns.append((vc * lax.rsqrt(var + EPS) * lng_ref[...] + lnb_ref[...]).astype(BF16))
            yield
        vn = jnp.concatenate(vns, axis=0)
        for n in range(tm // SG_CHUNK):
            rs = slice(n * SG_CHUNK, (n + 1) * SG_CHUNK)
            for g in range(SG_GROUPS):
                cs = slice(g * SG_GD, (g + 1) * SG_GD)
                u = zs[g // 2][rs, (g % 2) * SG_GD:(g % 2 + 1) * SG_GD]
                mixed = _dot(ws_ref[g], vn[rs, cs]) + bs_ref[:, cs]
                m_scr[rs, cs] = (u * mixed).astype(BF16)
            yield
        ys = []
        for j in range(2):
            ys.append(_dot(m_scr[...], wout_ref[:, j * 512:(j + 1) * 512]))
            yield
        yield from _residual_steps(x_ref, jnp.concatenate(ys, axis=-1), modm_ref[0, :, 2 * D:3 * D],
                                   ng_ref[1:2, :], x1_scr.at[i % 2])

    ffn = _ffn_steps(x1_scr.at[(i + 1) % 2], modf_ref, ng_ref, wa_ref, wb_ref, wo_ref, act_scr, y_ref)
    _interleave(ffn, mixer())


def _sgu_call(x, mod_l, ng_l, w_in, b_in, ln_g, ln_b, w_s, bs_full, w_out, wa, wb, wo):
    B, L, _ = x.shape
    tm = SGU_TILE
    n = (B * L) // tm
    per_seq = L // tm
    cur = lambda i: jnp.minimum(i, n - 1)
    prev = lambda i: jnp.maximum(i - 1, 0)
    out = pl.pallas_call(
        _sgu_kernel,
        grid=(n + 1,),
        in_specs=[
            pl.BlockSpec((tm, D), lambda i: (cur(i), 0)),
            pl.BlockSpec((1, 1, 6 * D), lambda i: (cur(i) // per_seq, 0, 0)),
            pl.BlockSpec((1, 1, 6 * D), lambda i: (prev(i) // per_seq, 0, 0)),
            _const_spec((4, D)),
            _const_spec((D, 2 * SG_WIDTH)),
            _const_spec((1, 2 * SG_WIDTH)),
            _const_spec((1, SG_WIDTH)),
            _const_spec((1, SG_WIDTH)),
            _const_spec((SG_GROUPS, SG_CHUNK, SG_CHUNK)),
            _const_spec((SG_CHUNK, SG_WIDTH)),
            _const_spec((SG_WIDTH, D)),
            _const_spec((N_FFN_CHUNKS, D, FFN_CHUNK)),
            _const_spec((N_FFN_CHUNKS, D, FFN_CHUNK)),
            _const_spec((FFN_HIDDEN, D)),
        ],
        out_specs=pl.BlockSpec((tm, D), lambda i: (prev(i), 0)),
        out_shape=jax.ShapeDtypeStruct((B * L, D), F32),
        scratch_shapes=[pltpu.VMEM((tm, SG_WIDTH), BF16), pltpu.VMEM((2, tm, D), F32),
                        pltpu.VMEM((tm, FFN_HIDDEN), BF16)],
        compiler_params=pltpu.CompilerParams(
            dimension_semantics=("arbitrary",), vmem_limit_bytes=VMEM_LIMIT),
        name="sgu_layer",
    )(x.reshape(B * L, D), mod_l, mod_l, ng_l, w_in, b_in, ln_g, ln_b, w_s, bs_full, w_out,
      wa, wb, wo)
    return out.reshape(B, L, D)


def _prep_ffn(ffn_w_in, ffn_w_out, i):
    w = ffn_w_in[i].astype(BF16)
    wa = w[:, :FFN_HIDDEN].reshape(D, N_FFN_CHUNKS, FFN_CHUNK).transpose(1, 0, 2)
    wb = w[:, FFN_HIDDEN:].reshape(D, N_FFN_CHUNKS, FFN_CHUNK).transpose(1, 0, 2)
    return wa, wb, ffn_w_out[i].astype(BF16)


def _prep_gate(gla_w_gk1, gla_w_gk2, j):
    wg1 = jnp.zeros((D, 2 * RANK_PAD), BF16)
    wg1 = wg1.at[:, :RANK].set(gla_w_gk1[j, 0].astype(BF16))
    wg1 = wg1.at[:, RANK_PAD:RANK_PAD + RANK].set(gla_w_gk1[j, 1].astype(BF16))
    wg2 = jnp.zeros((2, RANK_PAD, DK), BF16).at[:, :RANK, :].set(gla_w_gk2[j].astype(BF16))
    return wg1, wg2


def _trunk(x, mod, norm_g, gla_w_in, gla_w_gk1, gla_w_gk2, gla_b_gk, gla_g_head, gla_w_out,
           sg_w_in, sg_b_in, sg_ln_g, sg_ln_b, sg_w_s, sg_b_s, sg_w_out, ffn):
    B = x.shape[0]
    for i in range(DEPTH):
        mod_l = mod[i].reshape(B, 1, 6 * D)
        wa, wb, wo = ffn[i]
        j = i // 2
        if i % 2 == 0:
            wg1, wg2 = _prep_gate(gla_w_gk1, gla_w_gk2, j)
            q, k, v, r, g1b, o_f = _gla_a_call(
                x, mod_l, norm_g[i], gla_w_in[j].astype(BF16), wg1, wg2, gla_b_gk[j])
            x = _gla_b_call(x, mod_l, norm_g[i], q, k, v, r, g1b, o_f, wg2, gla_b_gk[j],
                            gla_g_head[j].reshape(1, HV), gla_w_out[j].astype(BF16), wa, wb, wo)
        else:
            bs_full = jnp.repeat(sg_b_s[j].T, SG_GD, axis=1)
            x = _sgu_call(x, mod_l, norm_g[i], sg_w_in[j].astype(BF16),
                          sg_b_in[j].reshape(1, -1), sg_ln_g[j].reshape(1, -1),
                          sg_ln_b[j].reshape(1, -1), sg_w_s[j].astype(BF16), bs_full,
                          sg_w_out[j].astype(BF16), wa, wb, wo)
    return x


def kernel(x_prompt, x_sample, c_prompt, c_sample, norm_g, w_ada, b_ada, gla_w_in, gla_w_gk1,
           gla_w_gk2, gla_b_gk, gla_g_head, gla_w_out, sg_w_in, sg_b_in, sg_ln_g, sg_ln_b,
           sg_w_s, sg_b_s, sg_w_out, ffn_w_in, ffn_w_out):
    bp, bs = c_prompt.shape[0], c_sample.shape[0]
    rows = -(-(bp + bs) // 16) * 16
    c_all = jnp.concatenate([c_prompt, c_sample, jnp.zeros((rows - bp - bs, D), F32)], axis=0)
    mod = _ada_call(c_all, w_ada, b_ada)
    ffn = [_prep_ffn(ffn_w_in, ffn_w_out, i) for i in range(DEPTH)]
    args = (norm_g, gla_w_in, gla_w_gk1, gla_w_gk2, gla_b_gk, gla_g_head, gla_w_out,
            sg_w_in, sg_b_in, sg_ln_g, sg_ln_b, sg_w_s, sg_b_s, sg_w_out, ffn)
    y_prompt = _trunk(x_prompt, mod[:, :bp], *args)
    y_sample = _trunk(x_sample, mod[:, bp:bp + bs], *args)
    return (y_prompt, y_sample)
```

```python
import functools

import jax
import jax.numpy as jnp
from jax import lax
from jax.experimental import pallas as pl
from jax.experimental.pallas import tpu as pltpu

D = 1024
DEPTH = 4
N_HEADS = 4
DK = 512
DV = 1024
HK = DK // N_HEADS
HV = DV // N_HEADS
RANK = 16
RANK_PAD = 128
TAU = 16.0
CHUNK = 64
SG_CHUNK = 128
SG_GROUPS = 4
SG_WIDTH = 1024
SG_GD = SG_WIDTH // SG_GROUPS
FFN_HIDDEN = 2816
FFN_CHUNK = 256
N_FFN_CHUNKS = FFN_HIDDEN // FFN_CHUNK
EPS = 1e-6

GLA_TILE = 512
SGU_TILE = 512
VMEM_LIMIT = 56 * 1024 * 1024

F32 = jnp.float32
BF16 = jnp.bfloat16


def _dot(a, b):
    return jnp.dot(a, b, preferred_element_type=F32)


def _dot_nt(a, b):
    return lax.dot_general(a, b, (((1,), (1,)), ((), ())), preferred_element_type=F32)


def _dot_tn(a, b):
    return lax.dot_general(a, b, (((0,), (0,)), ((), ())), preferred_element_type=F32)


def _sigmoid(x):
    return 1.0 / (1.0 + jnp.exp(-x))


def _rms(x, g):
    ms = jnp.mean(x * x, axis=-1, keepdims=True)
    return x * lax.rsqrt(ms + EPS) * g


def _log_sigmoid(x):
    return jnp.minimum(x, 0.0) - jnp.log1p(jnp.exp(-jnp.abs(x)))


def _split_bf16(x):
    hi = x.astype(BF16)
    lo = (x - hi.astype(F32)).astype(BF16)
    return hi, lo


def _const_spec(shape):
    zeros = (0,) * len(shape)
    return pl.BlockSpec(shape, lambda *_: zeros, pipeline_mode=pl.Buffered(1))


def _row_halves(n):
    return [slice(0, n // 2), slice(n // 2, n)]


def _interleave(*gens, pattern=""):
    live = list(gens)
    pattern = [int(ch) for ch in pattern if ch != " "]

    def advance(g):
        try:
            next(g)
        except StopIteration:
            live.remove(g)

    for idx in pattern:
        if gens[idx] in live:
            advance(gens[idx])
    while live:
        for g in list(live):
            advance(g)


def _ada_kernel(c_ref, w_ref, b_ref, o_ref):
    c = c_ref[...]
    s = (c * _sigmoid(c)).astype(BF16)
    o_ref[0] = _dot(s, w_ref[0].astype(BF16)) + b_ref[0]


def _ada_call(c_all, w_ada, b_ada):
    bp = c_all.shape[0]
    return pl.pallas_call(
        _ada_kernel,
        grid=(DEPTH, 6),
        in_specs=[
            pl.BlockSpec((bp, D), lambda i, j: (0, 0)),
            pl.BlockSpec((1, D, D), lambda i, j: (i, 0, j)),
            pl.BlockSpec((1, 1, D), lambda i, j: (i, 0, j)),
        ],
        out_specs=pl.BlockSpec((1, bp, D), lambda i, j: (i, 0, j)),
        out_shape=jax.ShapeDtypeStruct((DEPTH, bp, 6 * D), F32),
        name="ada_mod",
    )(c_all, w_ada, b_ada.reshape(DEPTH, 1, 6 * D))


def _adaln_steps(x_ref, g, sc, sh, out):
    gs = g * (1.0 + sc)
    for rs in _row_halves(x_ref.shape[0]):
        out.append((_rms(x_ref[rs, :], gs) + sh).astype(BF16))
        yield


def _residual_steps(x_ref, y, gain, g, dst_ref):
    gg = gain * g
    for rs in _row_halves(x_ref.shape[0]):
        dst_ref[rs, :] = x_ref[rs, :] + _rms(y[rs, :], gg)
        yield


def _ffn_steps(x1_ref, mod_ref, ng_ref, wab_ref, wo_ref, act_scr, y_ref):
    sh2 = mod_ref[0, :, 3 * D:4 * D]
    sc2 = mod_ref[0, :, 4 * D:5 * D]
    g2 = mod_ref[0, :, 5 * D:6 * D]
    hbs = []
    yield from _adaln_steps(x1_ref, ng_ref[2:3, :], sc2, sh2, hbs)
    hb = jnp.concatenate(hbs, axis=0)
    for c in range(N_FFN_CHUNKS):
        a = _dot(hb, wab_ref[:, c * FFN_CHUNK:(c + 1) * FFN_CHUNK])
        b = _dot(hb, wab_ref[:, FFN_HIDDEN + c * FFN_CHUNK:FFN_HIDDEN + (c + 1) * FFN_CHUNK])
        act_scr[:, c * FFN_CHUNK:(c + 1) * FFN_CHUNK] = (a * _sigmoid(a) * b).astype(BF16)
        yield
    ys = []
    for j in range(2):
        ys.append(_dot(act_scr[...], wo_ref[:, j * 512:(j + 1) * 512]))
        yield
    yield from _residual_steps(x1_ref, jnp.concatenate(ys, axis=-1), g2, ng_ref[3:4, :], y_ref)


def _chunk_decay_columns(rows):
    pad = jnp.zeros((128 - len(rows), 128), F32)
    m = jnp.concatenate(rows + [pad], axis=0)
    return jnp.exp(m.T)


def _decay_qk(q, k, b):
    return (q * jnp.exp(b)).astype(BF16), (k * jnp.exp(-b)).astype(BF16)


def _decay_chunk_end(k, b, *, reverse):
    nc = k.shape[0] // CHUNK
    k_e, rows = [], []
    for c in range(nc):
        r0 = c * CHUNK if reverse else c * CHUNK + CHUNK - 1
        tot = b[r0:r0 + 1, :]
        rs = slice(c * CHUNK, (c + 1) * CHUNK)
        k_e.append((k[rs, :] * jnp.exp(tot - b[rs, :])).astype(BF16))
        rows += [tot[:, h * HK:(h + 1) * HK] for h in range(N_HEADS)]
    return jnp.concatenate(k_e, axis=0), _chunk_decay_columns(rows)


def _scan_steps(qd_ref, kd_ref, ke_ref, v_ref, dec_ref, s_ref, out_fn, *, reverse):
    nc = qd_ref.shape[0] // CHUNK
    ri = lax.broadcasted_iota(jnp.int32, (CHUNK, CHUNK), 0)
    ci = lax.broadcasted_iota(jnp.int32, (CHUNK, CHUNK), 1)
    mask = (ci > ri) if reverse else (ci <= ri)
    dec = dec_ref[...]

    def independent(c):
        rs = slice(c * CHUNK, (c + 1) * CHUNK)
        res = []
        for h in range(N_HEADS):
            ks = slice(h * HK, (h + 1) * HK)
            vv = v_ref[rs, h * HV:(h + 1) * HV]
            res.append((_dot_nt(qd_ref[rs, ks], kd_ref[rs, ks]), _dot_tn(ke_ref[rs, ks], vv)))
        return res

    def dependent(c, res):
        rs = slice(c * CHUNK, (c + 1) * CHUNK)
        for h in range(N_HEADS):
            sc, u = res[h]
            vv = v_ref[rs, h * HV:(h + 1) * HV]
            s_old = s_ref[h]
            o = (_dot(jnp.where(mask, sc, 0.0).astype(BF16), vv)
                 + _dot(qd_ref[rs, h * HK:(h + 1) * HK], s_old.astype(BF16)))
            out_fn(c, h, o)
            j = c * N_HEADS + h
            s_ref[h] = s_old * dec[:, j:j + 1] + u

    order = list(range(nc - 1, -1, -1) if reverse else range(nc))
    prev = None
    for c in order:
        cur = (c, independent(c))
        if prev is not None:
            dependent(*prev)
        prev = cur
        yield
    dependent(*prev)
    yield


CUMSUM_BLOCK = 256


def _cumsum_matrix(reverse):
    r = jnp.arange(CUMSUM_BLOCK)[:, None]
    c = jnp.arange(CUMSUM_BLOCK)[None, :]
    tri = (c >= r) if reverse else (c <= r)
    return ((r // CHUNK == c // CHUNK) & tri).astype(BF16)


def _chunk_cumsum(t, hi, lo):
    blocks = []
    for r0 in range(0, hi.shape[0], CUMSUM_BLOCK):
        rs = slice(r0, r0 + CUMSUM_BLOCK)
        blocks.append(_dot(t, hi[rs, :]) + _dot(t, lo[rs, :]))
    return jnp.concatenate(blocks, axis=0)


def _gla_a_kernel(x_ref, mod_ref, ng_ref, t_ref, win_ref, wg1_ref, wg2_ref, bg_ref,
                  q_ref, k_ref, v_ref, r_ref, g1b_ref, of_ref,
                  s_ref, qd_scr, kd_scr, ke_scr, v_scr, dec_scr, *, per_seq):
    i = pl.program_id(0)
    wr = i % 2
    rd = (i + 1) % 2

    @pl.when(i == 0)
    def _():
        qd_scr[1] = jnp.zeros(qd_scr.shape[1:], BF16)
        kd_scr[1] = jnp.zeros(kd_scr.shape[1:], BF16)
        ke_scr[1] = jnp.zeros(ke_scr.shape[1:], BF16)
        v_scr[1] = jnp.zeros(v_scr.shape[1:], BF16)
        dec_scr[1] = jnp.zeros(dec_scr.shape[1:], F32)

    @pl.when((i == 0) | ((i - 1) % per_seq == 0))
    def _():
        s_ref[...] = jnp.zeros_like(s_ref)

    def proj():
        hbs = []
        yield from _adaln_steps(x_ref, ng_ref[0:1, :], mod_ref[0, :, D:2 * D], mod_ref[0, :, 0:D],
                                hbs)
        hb = jnp.concatenate(hbs, axis=0)
        g1 = _dot(hb, wg1_ref[...])
        g1b_ref[...] = g1[:, RANK_PAD:].astype(BF16)
        g1f = g1[:, :RANK_PAD].astype(BF16)
        for j in range(2):
            cs = slice(j * 512, (j + 1) * 512)
            vj = _dot(hb, win_ref[:, 2 * DK + j * 512:2 * DK + (j + 1) * 512]).astype(BF16)
            v_ref[:, cs] = vj
            v_scr[wr, :, cs] = vj
            if j == 0:
                gate = _dot(g1f, wg2_ref[0]) + bg_ref[0:1, :]
                la_hi, la_lo = _split_bf16(_log_sigmoid(gate) * (1.0 / TAU))
            yield
        b = _chunk_cumsum(t_ref[...], la_hi, la_lo)
        r0 = 2 * DK + DV
        r_ref[:, 0:512] = _dot(hb, win_ref[:, r0:r0 + 512]).astype(BF16)
        yield
        q = _dot(hb, win_ref[:, 0:DK]) * (HK ** -0.5)
        k = _dot(hb, win_ref[:, DK:2 * DK])
        r_ref[:, 512:1024] = _dot(hb, win_ref[:, r0 + 512:r0 + 1024]).astype(BF16)
        yield
        q_ref[...] = q.astype(BF16)
        k_ref[...] = k.astype(BF16)
        qd_scr[wr], kd_scr[wr] = _decay_qk(q, k, b)
        ke_scr[wr], dec_scr[wr] = _decay_chunk_end(k, b, reverse=False)

    def put(c, h, o):
        of_ref[c * CHUNK:(c + 1) * CHUNK, h * HV:(h + 1) * HV] = o.astype(BF16)

    scan = _scan_steps(qd_scr.at[rd], kd_scr.at[rd], ke_scr.at[rd], v_scr.at[rd], dec_scr.at[rd],
                       s_ref, put, reverse=False)
    _interleave(scan, proj())


def _gla_a_call(x, mod_l, ng_l, w_in, wg1, wg2, bg):
    B, L, _ = x.shape
    tm = GLA_TILE
    n = (B * L) // tm
    per_seq = L // tm
    cur = lambda i: jnp.minimum(i, n - 1)
    prev = lambda i: jnp.maximum(i - 1, 0)
    tok = lambda w, f: pl.BlockSpec((tm, w), lambda i: (f(i), 0))
    out_shapes = (
        jax.ShapeDtypeStruct((B * L, DK), BF16),
        jax.ShapeDtypeStruct((B * L, DK), BF16),
        jax.ShapeDtypeStruct((B * L, DV), BF16),
        jax.ShapeDtypeStruct((B * L, DV), BF16),
        jax.ShapeDtypeStruct((B * L, RANK_PAD), BF16),
        jax.ShapeDtypeStruct((B * L, DV), BF16),
    )
    return pl.pallas_call(
        functools.partial(_gla_a_kernel, per_seq=per_seq),
        grid=(n + 1,),
        in_specs=[
            tok(D, cur),
            pl.BlockSpec((1, 1, 6 * D), lambda i: (cur(i) // per_seq, 0, 0)),
            _const_spec((4, D)),
            _const_spec((CUMSUM_BLOCK, CUMSUM_BLOCK)),
            _const_spec((D, 2 * DK + 2 * DV)),
            _const_spec((D, 2 * RANK_PAD)),
            _const_spec((2, RANK_PAD, DK)),
            _const_spec((2, DK)),
        ],
        out_specs=(tok(DK, cur), tok(DK, cur), tok(DV, cur), tok(DV, cur), tok(RANK_PAD, cur),
                   tok(DV, prev)),
        out_shape=out_shapes,
        scratch_shapes=[
            pltpu.VMEM((N_HEADS, HK, HV), F32),
            pltpu.VMEM((2, tm, DK), BF16), pltpu.VMEM((2, tm, DK), BF16),
            pltpu.VMEM((2, tm, DK), BF16), pltpu.VMEM((2, tm, DV), BF16),
            pltpu.VMEM((2, 128, 128), F32),
        ],
        compiler_params=pltpu.CompilerParams(
            dimension_semantics=("arbitrary",), vmem_limit_bytes=VMEM_LIMIT),
        name="gla_a",
    )(x.reshape(B * L, D), mod_l, ng_l, _cumsum_matrix(False), w_in, wg1, wg2, bg)


def _gla_b_kernel(x_ref, modm_ref, modf_ref, ng_ref, t_ref, q_ref, k_ref, v_ref, r_ref, g1b_ref,
                  of_ref, wg2_ref, bg_ref, gh_ref, wout_ref, wab_ref, wo_ref,
                  y_ref, s_ref, o_scr, m_scr, qd_scr, kd_scr, ke_scr, dec_scr, x1_scr, act_scr,
                  *, per_seq):
    i = pl.program_id(0)

    @pl.when(i == 0)
    def _():
        x1_scr[1] = jnp.zeros(x1_scr.shape[1:], F32)

    @pl.when(i % per_seq == 0)
    def _():
        s_ref[...] = jnp.zeros_like(s_ref)

    def mixer():
        gate = _dot(g1b_ref[...], wg2_ref[1]) + bg_ref[1:2, :]
        la_hi, la_lo = _split_bf16(_log_sigmoid(gate) * (1.0 / TAU))
        yield
        b = _chunk_cumsum(t_ref[...], la_hi, la_lo)
        k = k_ref[...].astype(F32)
        qd_scr[...], kd_scr[...] = _decay_qk(q_ref[...].astype(F32), k, b)
        yield
        ke_scr[...], dec_scr[...] = _decay_chunk_end(k, b, reverse=True)
        yield

        def put(c, h, o):
            rs = slice(c * CHUNK, (c + 1) * CHUNK)
            vs = slice(h * HV, (h + 1) * HV)
            o_scr[rs, vs] = o + of_ref[rs, vs].astype(F32)

        yield from _scan_steps(qd_scr, kd_scr, ke_scr, v_ref, dec_scr, s_ref, put, reverse=True)
        gh = gh_ref[...]
        for h in range(N_HEADS):
            vs = slice(h * HV, (h + 1) * HV)
            rh = r_ref[:, vs].astype(F32)
            m_scr[:, vs] = (_rms(o_scr[:, vs], gh) * (rh * _sigmoid(rh))).astype(BF16)
            yield
        ys = []
        for j in range(2):
            ys.append(_dot(m_scr[...], wout_ref[:, j * 512:(j + 1) * 512]))
            yield
        yield from _residual_steps(x_ref, jnp.concatenate(ys, axis=-1), modm_ref[0, :, 2 * D:3 * D],
                                   ng_ref[1:2, :], x1_scr.at[i % 2])

    ffn = _ffn_steps(x1_scr.at[(i + 1) % 2], modf_ref, ng_ref, wab_ref, wo_ref, act_scr, y_ref)
    _interleave(ffn, mixer(), pattern="00")


def _gla_b_call(x, mod_l, ng_l, q, k, v, r, g1b, o_f, wg2, bg, g_head, w_out, wab, wo):
    B, L, _ = x.shape
    tm = GLA_TILE
    n = (B * L) // tm
    per_seq = L // tm
    tile = lambda j: (j // per_seq) * per_seq + (per_seq - 1 - j % per_seq)
    cur = lambda i: jnp.minimum(i, n - 1)
    prev = lambda i: jnp.maximum(i - 1, 0)
    tok = lambda w: pl.BlockSpec((tm, w), lambda i: (tile(cur(i)), 0))
    out = pl.pallas_call(
        functools.partial(_gla_b_kernel, per_seq=per_seq),
        grid=(n + 1,),
        in_specs=[
            tok(D),
            pl.BlockSpec((1, 1, 6 * D), lambda i: (cur(i) // per_seq, 0, 0)),
            pl.BlockSpec((1, 1, 6 * D), lambda i: (prev(i) // per_seq, 0, 0)),
            _const_spec((4, D)),
            _const_spec((CUMSUM_BLOCK, CUMSUM_BLOCK)),
            tok(DK), tok(DK), tok(DV), tok(DV), tok(RANK_PAD), tok(DV),
            _const_spec((2, RANK_PAD, DK)),
            _const_spec((2, DK)),
            _const_spec((1, HV)),
            _const_spec((DV, D)),
            _const_spec((D, 2 * FFN_HIDDEN)),
            _const_spec((FFN_HIDDEN, D)),
        ],
        out_specs=pl.BlockSpec((tm, D), lambda i: (tile(prev(i)), 0)),
        out_shape=jax.ShapeDtypeStruct((B * L, D), F32),
        scratch_shapes=[
            pltpu.VMEM((N_HEADS, HK, HV), F32),
            pltpu.VMEM((tm, DV), F32),
            pltpu.VMEM((tm, DV), BF16),
            pltpu.VMEM((tm, DK), BF16), pltpu.VMEM((tm, DK), BF16), pltpu.VMEM((tm, DK), BF16),
            pltpu.VMEM((128, 128), F32),
            pltpu.VMEM((2, tm, D), F32),
            pltpu.VMEM((tm, FFN_HIDDEN), BF16),
        ],
        compiler_params=pltpu.CompilerParams(
            dimension_semantics=("arbitrary",), vmem_limit_bytes=VMEM_LIMIT),
        name="gla_b",
    )(x.reshape(B * L, D), mod_l, mod_l, ng_l, _cumsum_matrix(True), q, k, v, r, g1b, o_f,
      wg2, bg, g_head, w_out, wab, wo)
    return out.reshape(B, L, D)


GELU_C = 0.7978845608028654


def _gelu_tanh(x):
    return x * (0.5 + 0.5 * jnp.tanh(x * (GELU_C + (GELU_C * 0.044715) * (x * x))))


def _sgu_kernel(x_ref, modm_ref, modf_ref, ng_ref, win_ref, bin_ref, lng_ref, lnb_ref, ws_ref,
                bs_ref, wout_ref, wab_ref, wo_ref, y_ref, m_scr, x1_scr, act_scr):
    i = pl.program_id(0)
    tm = x_ref.shape[0]

    @pl.when(i == 0)
    def _():
        x1_scr[1] = jnp.zeros((tm, D), F32)

    def mixer():
        hbs = []
        yield from _adaln_steps(x_ref, ng_ref[0:1, :], modm_ref[0, :, D:2 * D],
                                modm_ref[0, :, 0:D], hbs)
        hb = jnp.concatenate(hbs, axis=0)
        zs = []
        for j in range(4):
            cs = slice(j * 512, (j + 1) * 512)
            zs.append(_gelu_tanh(_dot(hb, win_ref[:, cs]) + bin_ref[:, cs]))
            yield
        vns = []
        for rs in _row_halves(tm):
            v = jnp.concatenate([zs[2][rs, :], zs[3][rs, :]], axis=-1)
            mu = jnp.mean(v, axis=-1, keepdims=True)
            vc = v - mu
            var = jnp.mean(vc * vc, axis=-1, keepdims=True)
            vns.append((vc * lax.rsqrt(var + EPS) * lng_ref[...] + lnb_ref[...]).astype(BF16))
            yield
        vn = jnp.concatenate(vns, axis=0)
        for n in range(tm // SG_CHUNK):
            rs = slice(n * SG_CHUNK, (n + 1) * SG_CHUNK)
            for g in range(SG_GROUPS):
                cs = slice(g * SG_GD, (g + 1) * SG_GD)
                u = zs[g // 2][rs, (g % 2) * SG_GD:(g % 2 + 1) * SG_GD]
                mixed = _dot(ws_ref[g], vn[rs, cs]) + bs_ref[:, cs]
                m_scr[rs, cs] = (u * mixed).astype(BF16)
            yield
        ys = []
        for j in range(2):
            ys.append(_dot(m_scr[...], wout_ref[:, j * 512:(j + 1) * 512]))
            yield
        yield from _residual_steps(x_ref, jnp.concatenate(ys, axis=-1), modm_ref[0, :, 2 * D:3 * D],
                                   ng_ref[1:2, :], x1_scr.at[i % 2])

    ffn = _ffn_steps(x1_scr.at[(i + 1) % 2], modf_ref, ng_ref, wab_ref, wo_ref, act_scr, y_ref)
    _interleave(ffn, mixer())


def _sgu_call(x, mod_l, ng_l, w_in, b_in, ln_g, ln_b, w_s, bs_full, w_out, wab, wo):
    B, L, _ = x.shape
    tm = SGU_TILE
    n = (B * L) // tm
    per_seq = L // tm
    cur = lambda i: jnp.minimum(i, n - 1)
    prev = lambda i: jnp.maximum(i - 1, 0)
    out = pl.pallas_call(
        _sgu_kernel,
        grid=(n + 1,),
        in_specs=[
            pl.BlockSpec((tm, D), lambda i: (cur(i), 0)),
            pl.BlockSpec((1, 1, 6 * D), lambda i: (cur(i) // per_seq, 0, 0)),
            pl.BlockSpec((1, 1, 6 * D), lambda i: (prev(i) // per_seq, 0, 0)),
            _const_spec((4, D)),
            _const_spec((D, 2 * SG_WIDTH)),
            _const_spec((1, 2 * SG_WIDTH)),
            _const_spec((1, SG_WIDTH)),
            _const_spec((1, SG_WIDTH)),
            _const_spec((SG_GROUPS, SG_CHUNK, SG_CHUNK)),
            _const_spec((SG_CHUNK, SG_WIDTH)),
            _const_spec((SG_WIDTH, D)),
            _const_spec((D, 2 * FFN_HIDDEN)),
            _const_spec((FFN_HIDDEN, D)),
        ],
        out_specs=pl.BlockSpec((tm, D), lambda i: (prev(i), 0)),
        out_shape=jax.ShapeDtypeStruct((B * L, D), F32),
        scratch_shapes=[pltpu.VMEM((tm, SG_WIDTH), BF16), pltpu.VMEM((2, tm, D), F32),
                        pltpu.VMEM((tm, FFN_HIDDEN), BF16)],
        compiler_params=pltpu.CompilerParams(
            dimension_semantics=("arbitrary",), vmem_limit_bytes=VMEM_LIMIT),
        name="sgu_layer",
    )(x.reshape(B * L, D), mod_l, mod_l, ng_l, w_in, b_in, ln_g, ln_b, w_s, bs_full, w_out,
      wab, wo)
    return out.reshape(B, L, D)


def _prep_ffn(ffn_w_in, ffn_w_out, i):
    return ffn_w_in[i].astype(BF16), ffn_w_out[i].astype(BF16)


def _prep_gate(gla_w_gk1, gla_w_gk2, j):
    wg1 = jnp.zeros((D, 2 * RANK_PAD), BF16)
    wg1 = wg1.at[:, :RANK].set(gla_w_gk1[j, 0].astype(BF16))
    wg1 = wg1.at[:, RANK_PAD:RANK_PAD + RANK].set(gla_w_gk1[j, 1].astype(BF16))
    wg2 = jnp.zeros((2, RANK_PAD, DK), BF16).at[:, :RANK, :].set(gla_w_gk2[j].astype(BF16))
    return wg1, wg2


def _trunk(x, mod, norm_g, gla_w_in, gla_w_gk1, gla_w_gk2, gla_b_gk, gla_g_head, gla_w_out,
           sg_w_in, sg_b_in, sg_ln_g, sg_ln_b, sg_w_s, sg_b_s, sg_w_out, ffn):
    B = x.shape[0]
    for i in range(DEPTH):
        mod_l = mod[i].reshape(B, 1, 6 * D)
        wab, wo = ffn[i]
        j = i // 2
        if i % 2 == 0:
            wg1, wg2 = _prep_gate(gla_w_gk1, gla_w_gk2, j)
            q, k, v, r, g1b, o_f = _gla_a_call(
                x, mod_l, norm_g[i], gla_w_in[j].astype(BF16), wg1, wg2, gla_b_gk[j])
            x = _gla_b_call(x, mod_l, norm_g[i], q, k, v, r, g1b, o_f, wg2, gla_b_gk[j],
                            gla_g_head[j].reshape(1, HV), gla_w_out[j].astype(BF16), wab, wo)
        else:
            bs_full = jnp.repeat(sg_b_s[j].T, SG_GD, axis=1)
            x = _sgu_call(x, mod_l, norm_g[i], sg_w_in[j].astype(BF16),
                          sg_b_in[j].reshape(1, -1), sg_ln_g[j].reshape(1, -1),
                          sg_ln_b[j].reshape(1, -1), sg_w_s[j].astype(BF16), bs_full,
                          sg_w_out[j].astype(BF16), wab, wo)
    return x


def kernel(x_prompt, x_sample, c_prompt, c_sample, norm_g, w_ada, b_ada, gla_w_in, gla_w_gk1,
           gla_w_gk2, gla_b_gk, gla_g_head, gla_w_out, sg_w_in, sg_b_in, sg_ln_g, sg_ln_b,
           sg_w_s, sg_b_s, sg_w_out, ffn_w_in, ffn_w_out):
    bp, bs = c_prompt.shape[0], c_sample.shape[0]
    rows = -(-(bp + bs) // 16) * 16
    c_all = jnp.concatenate([c_prompt, c_sample, jnp.zeros((rows - bp - bs, D), F32)], axis=0)
    mod = _ada_call(c_all, w_ada, b_ada)
    ffn = [_prep_ffn(ffn_w_in, ffn_w_out, i) for i in range(DEPTH)]
    args = (norm_g, gla_w_in, gla_w_gk1, gla_w_gk2, gla_b_gk, gla_g_head, gla_w_out,
            sg_w_in, sg_b_in, sg_ln_g, sg_ln_b, sg_w_s, sg_b_s, sg_w_out, ffn)
    y_prompt = _trunk(x_prompt, mod[:, :bp], *args)
    y_sample = _trunk(x_sample, mod[:, bp:bp + bs], *args)
    return (y_prompt, y_sample)
```

```python
import functools

import jax
import jax.numpy as jnp
from jax import lax
from jax.experimental import pallas as pl
from jax.experimental.pallas import tpu as pltpu

D = 1024
DEPTH = 4
N_HEADS = 4
DK = 512
DV = 1024
HK = DK // N_HEADS
HV = DV // N_HEADS
RANK = 16
RANK_PAD = 128
TAU = 16.0
CHUNK = 64
SG_CHUNK = 128
SG_GROUPS = 4
SG_WIDTH = 1024
SG_GD = SG_WIDTH // SG_GROUPS
FFN_HIDDEN = 2816
FFN_CHUNK = 256
EPS = 1e-6

GLA_A_TILE = 1024
GLA_TILE = 512
SGU_TILE = 512
VMEM_LIMIT = 56 * 1024 * 1024
GLA_B_FFN_SPLIT = 5

F32 = jnp.float32
BF16 = jnp.bfloat16


def _dot(a, b):
    return jnp.dot(a, b, preferred_element_type=F32)


def _dot_nt(a, b):
    return lax.dot_general(a, b, (((1,), (1,)), ((), ())), preferred_element_type=F32)


def _dot_tn(a, b):
    return lax.dot_general(a, b, (((0,), (0,)), ((), ())), preferred_element_type=F32)


def _sigmoid(x):
    return 1.0 / (1.0 + jnp.exp(-x))


def _rms(x, g):
    ms = jnp.mean(x * x, axis=-1, keepdims=True)
    return x * lax.rsqrt(ms + EPS) * g


def _log_sigmoid(x):
    return jnp.minimum(x, 0.0) - jnp.log1p(jnp.exp(-jnp.abs(x)))


def _split_bf16(x):
    hi = x.astype(BF16)
    lo = (x - hi.astype(F32)).astype(BF16)
    return hi, lo


def _const_spec(shape):
    zeros = (0,) * len(shape)
    return pl.BlockSpec(shape, lambda *_: zeros, pipeline_mode=pl.Buffered(1))


def _row_halves(n):
    return [slice(0, n // 2), slice(n // 2, n)]


def _interleave(*gens, pattern=""):
    live = list(gens)
    pattern = [int(ch) for ch in pattern if ch != " "]

    def advance(g):
        try:
            next(g)
        except StopIteration:
            live.remove(g)

    for idx in pattern:
        if gens[idx] in live:
            advance(gens[idx])
    while live:
        for g in list(live):
            advance(g)


def _ada_kernel(c_ref, w_ref, b_ref, o_ref):
    c = c_ref[...]
    s = (c * _sigmoid(c)).astype(BF16)
    o_ref[0] = _dot(s, w_ref[0].astype(BF16)) + b_ref[0]


def _ada_call(c_all, w_ada, b_ada):
    bp = c_all.shape[0]
    return pl.pallas_call(
        _ada_kernel,
        grid=(DEPTH, 6),
        in_specs=[
            pl.BlockSpec((bp, D), lambda i, j: (0, 0)),
            pl.BlockSpec((1, D, D), lambda i, j: (i, 0, j)),
            pl.BlockSpec((1, 1, D), lambda i, j: (i, 0, j)),
        ],
        out_specs=pl.BlockSpec((1, bp, D), lambda i, j: (i, 0, j)),
        out_shape=jax.ShapeDtypeStruct((DEPTH, bp, 6 * D), F32),
        name="ada_mod",
    )(c_all, w_ada, b_ada.reshape(DEPTH, 1, 6 * D))


def _adaln_steps(x_ref, g, sc, sh, out):
    gs = g * (1.0 + sc)
    for rs in _row_halves(x_ref.shape[0]):
        out.append((_rms(x_ref[rs, :], gs) + sh).astype(BF16))
        yield


def _residual_steps(x_ref, y, gain, g, dst_ref):
    gg = gain * g
    for rs in _row_halves(x_ref.shape[0]):
        dst_ref[rs, :] = x_ref[rs, :] + _rms(y[rs, :], gg)
        yield


def _ffn_steps(x1_ref, mod_ref, ng_ref, wab_ref, wo_ref, act_scr, y_ref, split_chunks=0):
    sh2 = mod_ref[0, :, 3 * D:4 * D]
    sc2 = mod_ref[0, :, 4 * D:5 * D]
    g2 = mod_ref[0, :, 5 * D:6 * D]
    hbs = []
    yield from _adaln_steps(x1_ref, ng_ref[2:3, :], sc2, sh2, hbs)
    hb = jnp.concatenate(hbs, axis=0)
    for c0 in range(0, FFN_HIDDEN, FFN_CHUNK):
        c1 = min(c0 + FFN_CHUNK, FFN_HIDDEN)
        a = _dot(hb, wab_ref[:, c0:c1])
        if c0 // FFN_CHUNK < split_chunks:
            yield
        b = _dot(hb, wab_ref[:, FFN_HIDDEN + c0:FFN_HIDDEN + c1])
        act_scr[:, c0:c1] = (a * _sigmoid(a) * b).astype(BF16)
        yield
    ys = []
    for j in range(2):
        ys.append(_dot(act_scr[...], wo_ref[:, j * 512:(j + 1) * 512]))
        yield
    yield from _residual_steps(x1_ref, jnp.concatenate(ys, axis=-1), g2, ng_ref[3:4, :], y_ref)


def _chunk_decay_columns(rows):
    pad = jnp.zeros((128 - len(rows), 128), F32)
    m = jnp.concatenate(rows + [pad], axis=0)
    return jnp.exp(m.T)


def _decay_qk(q, k, b):
    return (q * jnp.exp(b)).astype(BF16), (k * jnp.exp(-b)).astype(BF16)


def _decay_chunk_end(k, b, *, reverse):
    nc = k.shape[0] // CHUNK
    k_e, rows = [], []
    for c in range(nc):
        r0 = c * CHUNK if reverse else c * CHUNK + CHUNK - 1
        tot = b[r0:r0 + 1, :]
        rs = slice(c * CHUNK, (c + 1) * CHUNK)
        k_e.append((k[rs, :] * jnp.exp(tot - b[rs, :])).astype(BF16))
        rows += [tot[:, h * HK:(h + 1) * HK] for h in range(N_HEADS)]
    return jnp.concatenate(k_e, axis=0), _chunk_decay_columns(rows)


def _scan_steps(qd_ref, kd_ref, ke_ref, v_ref, dec_ref, s_ref, out_fn, *, reverse):
    nc = qd_ref.shape[0] // CHUNK
    ri = lax.broadcasted_iota(jnp.int32, (CHUNK, CHUNK), 0)
    ci = lax.broadcasted_iota(jnp.int32, (CHUNK, CHUNK), 1)
    mask = (ci > ri) if reverse else (ci <= ri)
    dec = dec_ref[...]

    def independent(c):
        rs = slice(c * CHUNK, (c + 1) * CHUNK)
        res = []
        for h in range(N_HEADS):
            ks = slice(h * HK, (h + 1) * HK)
            vv = v_ref[rs, h * HV:(h + 1) * HV]
            res.append((_dot_nt(qd_ref[rs, ks], kd_ref[rs, ks]), _dot_tn(ke_ref[rs, ks], vv)))
        return res

    def dependent(c, res):
        rs = slice(c * CHUNK, (c + 1) * CHUNK)
        for h in range(N_HEADS):
            sc, u = res[h]
            vv = v_ref[rs, h * HV:(h + 1) * HV]
            s_old = s_ref[h]
            o = (_dot(jnp.where(mask, sc, 0.0).astype(BF16), vv)
                 + _dot(qd_ref[rs, h * HK:(h + 1) * HK], s_old.astype(BF16)))
            out_fn(c, h, o)
            j = c * N_HEADS + h
            s_ref[h] = s_old * dec[:, j:j + 1] + u

    order = list(range(nc - 1, -1, -1) if reverse else range(nc))
    prev = None
    for c in order:
        cur = (c, independent(c))
        if prev is not None:
            dependent(*prev)
        prev = cur
        yield
    dependent(*prev)
    yield


CUMSUM_BLOCK = 256


def _cumsum_matrix(reverse):
    r = jnp.arange(CUMSUM_BLOCK)[:, None]
    c = jnp.arange(CUMSUM_BLOCK)[None, :]
    tri = (c >= r) if reverse else (c <= r)
    return ((r // CHUNK == c // CHUNK) & tri).astype(BF16)


def _chunk_cumsum(t, hi, lo):
    blocks = []
    for r0 in range(0, hi.shape[0], CUMSUM_BLOCK):
        rs = slice(r0, r0 + CUMSUM_BLOCK)
        blocks.append(_dot(t, hi[rs, :]) + _dot(t, lo[rs, :]))
    return jnp.concatenate(blocks, axis=0)


def _gla_a_kernel(x_ref, mod_ref, ng_ref, t_ref, win_ref, wg1_ref, wg2_ref, bg_ref,
                  q_ref, k_ref, v_ref, r_ref, g1b_ref, of_ref,
                  s_ref, qd_scr, kd_scr, ke_scr, v_scr, dec_scr, *, per_seq):
    i = pl.program_id(0)
    wr = i % 2
    rd = (i + 1) % 2

    @pl.when(i == 0)
    def _():
        qd_scr[1] = jnp.zeros(qd_scr.shape[1:], BF16)
        kd_scr[1] = jnp.zeros(kd_scr.shape[1:], BF16)
        ke_scr[1] = jnp.zeros(ke_scr.shape[1:], BF16)
        v_scr[1] = jnp.zeros(v_scr.shape[1:], BF16)
        dec_scr[1] = jnp.zeros(dec_scr.shape[1:], F32)

    @pl.when((i == 0) | ((i - 1) % per_seq == 0))
    def _():
        s_ref[...] = jnp.zeros_like(s_ref)

    def proj():
        hbs = []
        yield from _adaln_steps(x_ref, ng_ref[0:1, :], mod_ref[0, :, D:2 * D], mod_ref[0, :, 0:D],
                                hbs)
        hb = jnp.concatenate(hbs, axis=0)
        g1 = _dot(hb, wg1_ref[...])
        g1b_ref[...] = g1[:, RANK_PAD:].astype(BF16)
        g1f = g1[:, :RANK_PAD].astype(BF16)
        for j in range(2):
            cs = slice(j * 512, (j + 1) * 512)
            vj = _dot(hb, win_ref[:, 2 * DK + j * 512:2 * DK + (j + 1) * 512]).astype(BF16)
            v_ref[:, cs] = vj
            v_scr[wr, :, cs] = vj
            if j == 0:
                gate = _dot(g1f, wg2_ref[0]) + bg_ref[0:1, :]
                la_hi, la_lo = _split_bf16(_log_sigmoid(gate) * (1.0 / TAU))
            yield
        b = _chunk_cumsum(t_ref[...], la_hi, la_lo)
        r0 = 2 * DK + DV
        r_ref[:, 0:512] = _dot(hb, win_ref[:, r0:r0 + 512]).astype(BF16)
        yield
        q = _dot(hb, win_ref[:, 0:DK]) * (HK ** -0.5)
        k = _dot(hb, win_ref[:, DK:2 * DK])
        r_ref[:, 512:1024] = _dot(hb, win_ref[:, r0 + 512:r0 + 1024]).astype(BF16)
        yield
        q_ref[...] = q.astype(BF16)
        k_ref[...] = k.astype(BF16)
        qd_scr[wr], kd_scr[wr] = _decay_qk(q, k, b)
        yield
        ke_scr[wr], dec_scr[wr] = _decay_chunk_end(k, b, reverse=False)

    def put(c, h, o):
        of_ref[c * CHUNK:(c + 1) * CHUNK, h * HV:(h + 1) * HV] = o.astype(BF16)

    scan = _scan_steps(qd_scr.at[rd], kd_scr.at[rd], ke_scr.at[rd], v_scr.at[rd], dec_scr.at[rd],
                       s_ref, put, reverse=False)
    _interleave(scan, proj())


def _gla_a_call(x, mod_l, ng_l, w_in, wg1, wg2, bg):
    B, L, _ = x.shape
    tm = GLA_A_TILE
    n = (B * L) // tm
    per_seq = L // tm
    cur = lambda i: jnp.minimum(i, n - 1)
    prev = lambda i: jnp.maximum(i - 1, 0)
    tok = lambda w, f: pl.BlockSpec((tm, w), lambda i: (f(i), 0))
    out_shapes = (
        jax.ShapeDtypeStruct((B * L, DK), BF16),
        jax.ShapeDtypeStruct((B * L, DK), BF16),
        jax.ShapeDtypeStruct((B * L, DV), BF16),
        jax.ShapeDtypeStruct((B * L, DV), BF16),
        jax.ShapeDtypeStruct((B * L, RANK_PAD), BF16),
        jax.ShapeDtypeStruct((B * L, DV), BF16),
    )
    return pl.pallas_call(
        functools.partial(_gla_a_kernel, per_seq=per_seq),
        grid=(n + 1,),
        in_specs=[
            tok(D, cur),
            pl.BlockSpec((1, 1, 6 * D), lambda i: (cur(i) // per_seq, 0, 0)),
            _const_spec((4, D)),
            _const_spec((CUMSUM_BLOCK, CUMSUM_BLOCK)),
            _const_spec((D, 2 * DK + 2 * DV)),
            _const_spec((D, 2 * RANK_PAD)),
            _const_spec((2, RANK_PAD, DK)),
            _const_spec((2, DK)),
        ],
        out_specs=(tok(DK, cur), tok(DK, cur), tok(DV, cur), tok(DV, cur), tok(RANK_PAD, cur),
                   tok(DV, prev)),
        out_shape=out_shapes,
        scratch_shapes=[
            pltpu.VMEM((N_HEADS, HK, HV), F32),
            pltpu.VMEM((2, tm, DK), BF16), pltpu.VMEM((2, tm, DK), BF16),
            pltpu.VMEM((2, tm, DK), BF16), pltpu.VMEM((2, tm, DV), BF16),
            pltpu.VMEM((2, 128, 128), F32),
        ],
        compiler_params=pltpu.CompilerParams(
            dimension_semantics=("arbitrary",), vmem_limit_bytes=VMEM_LIMIT),
        name="gla_a",
    )(x.reshape(B * L, D), mod_l, ng_l, _cumsum_matrix(False), w_in, wg1, wg2, bg)


def _gla_b_kernel(x_ref, modm_ref, modf_ref, ng_ref, t_ref, q_ref, k_ref, v_ref, r_ref, g1b_ref,
                  of_ref, wg2_ref, bg_ref, gh_ref, wout_ref, wab_ref, wo_ref,
                  y_ref, s_ref, o_scr, m_scr, qd_scr, kd_scr, ke_scr, dec_scr, x1_scr, act_scr,
                  *, per_seq):
    i = pl.program_id(0)

    @pl.when(i == 0)
    def _():
        x1_scr[1] = jnp.zeros(x1_scr.shape[1:], F32)

    @pl.when(i % per_seq == 0)
    def _():
        s_ref[...] = jnp.zeros_like(s_ref)

    def mixer():
        gate = _dot(g1b_ref[...], wg2_ref[1]) + bg_ref[1:2, :]
        la_hi, la_lo = _split_bf16(_log_sigmoid(gate) * (1.0 / TAU))
        yield
        b = _chunk_cumsum(t_ref[...], la_hi, la_lo)
        k = k_ref[...].astype(F32)
        qd_scr[...], kd_scr[...] = _decay_qk(q_ref[...].astype(F32), k, b)
        yield
        ke_scr[...], dec_scr[...] = _decay_chunk_end(k, b, reverse=True)
        yield

        def put(c, h, o):
            rs = slice(c * CHUNK, (c + 1) * CHUNK)
            vs = slice(h * HV, (h + 1) * HV)
            o_scr[rs, vs] = o + of_ref[rs, vs].astype(F32)

        yield from _scan_steps(qd_scr, kd_scr, ke_scr, v_ref, dec_scr, s_ref, put, reverse=True)
        gh = gh_ref[...]
        for h in range(N_HEADS):
            vs = slice(h * HV, (h + 1) * HV)
            rh = r_ref[:, vs].astype(F32)
            m_scr[:, vs] = (_rms(o_scr[:, vs], gh) * (rh * _sigmoid(rh))).astype(BF16)
            yield
        ys = []
        for j in range(2):
            ys.append(_dot(m_scr[...], wout_ref[:, j * 512:(j + 1) * 512]))
            yield
        yield from _residual_steps(x_ref, jnp.concatenate(ys, axis=-1), modm_ref[0, :, 2 * D:3 * D],
                                   ng_ref[1:2, :], x1_scr.at[i % 2])

    ffn = _ffn_steps(x1_scr.at[(i + 1) % 2], modf_ref, ng_ref, wab_ref, wo_ref, act_scr, y_ref,
                     split_chunks=GLA_B_FFN_SPLIT)
    _interleave(ffn, mixer(), pattern="00")


def _gla_b_call(x, mod_l, ng_l, q, k, v, r, g1b, o_f, wg2, bg, g_head, w_out, wab, wo):
    B, L, _ = x.shape
    tm = GLA_TILE
    n = (B * L) // tm
    per_seq = L // tm
    tile = lambda j: (j // per_seq) * per_seq + (per_seq - 1 - j % per_seq)
    cur = lambda i: jnp.minimum(i, n - 1)
    prev = lambda i: jnp.maximum(i - 1, 0)
    tok = lambda w: pl.BlockSpec((tm, w), lambda i: (tile(cur(i)), 0))
    out = pl.pallas_call(
        functools.partial(_gla_b_kernel, per_seq=per_seq),
        grid=(n + 1,),
        in_specs=[
            tok(D),
            pl.BlockSpec((1, 1, 6 * D), lambda i: (cur(i) // per_seq, 0, 0)),
            pl.BlockSpec((1, 1, 6 * D), lambda i: (prev(i) // per_seq, 0, 0)),
            _const_spec((4, D)),
            _const_spec((CUMSUM_BLOCK, CUMSUM_BLOCK)),
            tok(DK), tok(DK), tok(DV), tok(DV), tok(RANK_PAD), tok(DV),
            _const_spec((2, RANK_PAD, DK)),
            _const_spec((2, DK)),
            _const_spec((1, HV)),
            _const_spec((DV, D)),
            _const_spec((D, 2 * FFN_HIDDEN)),
            _const_spec((FFN_HIDDEN, D)),
        ],
        out_specs=pl.BlockSpec((tm, D), lambda i: (tile(prev(i)), 0)),
        out_shape=jax.ShapeDtypeStruct((B * L, D), F32),
        scratch_shapes=[
            pltpu.VMEM((N_HEADS, HK, HV), F32),
            pltpu.VMEM((tm, DV), F32),
            pltpu.VMEM((tm, DV), BF16),
            pltpu.VMEM((tm, DK), BF16), pltpu.VMEM((tm, DK), BF16), pltpu.VMEM((tm, DK), BF16),
            pltpu.VMEM((128, 128), F32),
            pltpu.VMEM((2, tm, D), F32),
            pltpu.VMEM((tm, FFN_HIDDEN), BF16),
        ],
        compiler_params=pltpu.CompilerParams(
            dimension_semantics=("arbitrary",), vmem_limit_bytes=VMEM_LIMIT),
        name="gla_b",
    )(x.reshape(B * L, D), mod_l, mod_l, ng_l, _cumsum_matrix(True), q, k, v, r, g1b, o_f,
      wg2, bg, g_head, w_out, wab, wo)
    return out.reshape(B, L, D)


GELU_C = 0.7978845608028654


def _gelu_tanh(x):
    return x * (0.5 + 0.5 * jnp.tanh(x * (GELU_C + (GELU_C * 0.044715) * (x * x))))


def _sgu_kernel(x_ref, modm_ref, modf_ref, ng_ref, win_ref, bin_ref, lng_ref, lnb_ref, ws_ref,
                bs_ref, wout_ref, wab_ref, wo_ref, y_ref, m_scr, x1_scr, act_scr):
    i = pl.program_id(0)
    tm = x_ref.shape[0]

    @pl.when(i == 0)
    def _():
        x1_scr[1] = jnp.zeros((tm, D), F32)

    def mixer():
        hbs = []
        yield from _adaln_steps(x_ref, ng_ref[0:1, :], modm_ref[0, :, D:2 * D],
                                modm_ref[0, :, 0:D], hbs)
        hb = jnp.concatenate(hbs, axis=0)
        zs = []
        for j in range(4):
            cs = slice(j * 512, (j + 1) * 512)
            zs.append(_gelu_tanh(_dot(hb, win_ref[:, cs]) + bin_ref[:, cs]))
            yield
        vns = []
        for rs in _row_halves(tm):
            v = jnp.concatenate([zs[2][rs, :], zs[3][rs, :]], axis=-1)
            mu = jnp.mean(v, axis=-1, keepdims=True)
            vc = v - mu
            var = jnp.mean(vc * vc, axis=-1, keepdims=True)
            vns.append((vc * lax.rsqrt(var + EPS) * lng_ref[...] + lnb_ref[...]).astype(BF16))
            yield
        vn = jnp.concatenate(vns, axis=0)
        for n in range(tm // SG_CHUNK):
            rs = slice(n * SG_CHUNK, (n + 1) * SG_CHUNK)
            for g in range(SG_GROUPS):
                cs = slice(g * SG_GD, (g + 1) * SG_GD)
                u = zs[g // 2][rs, (g % 2) * SG_GD:(g % 2 + 1) * SG_GD]
                mixed = _dot(ws_ref[g], vn[rs, cs]) + bs_ref[:, cs]
                m_scr[rs, cs] = (u * mixed).astype(BF16)
            yield
        ys = []
        for j in range(2):
            ys.append(_dot(m_scr[...], wout_ref[:, j * 512:(j + 1) * 512]))
            yield
        yield from _residual_steps(x_ref, jnp.concatenate(ys, axis=-1), modm_ref[0, :, 2 * D:3 * D],
                                   ng_ref[1:2, :], x1_scr.at[i % 2])

    ffn = _ffn_steps(x1_scr.at[(i + 1) % 2], modf_ref, ng_ref, wab_ref, wo_ref, act_scr, y_ref)
    _interleave(ffn, mixer())


def _sgu_call(x, mod_l, ng_l, w_in, b_in, ln_g, ln_b, w_s, bs_full, w_out, wab, wo):
    B, L, _ = x.shape
    tm = SGU_TILE
    n = (B * L) // tm
    per_seq = L // tm
    cur = lambda i: jnp.minimum(i, n - 1)
    prev = lambda i: jnp.maximum(i - 1, 0)
    out = pl.pallas_call(
        _sgu_kernel,
        grid=(n + 1,),
        in_specs=[
            pl.BlockSpec((tm, D), lambda i: (cur(i), 0)),
            pl.BlockSpec((1, 1, 6 * D), lambda i: (cur(i) // per_seq, 0, 0)),
            pl.BlockSpec((1, 1, 6 * D), lambda i: (prev(i) // per_seq, 0, 0)),
            _const_spec((4, D)),
            _const_spec((D, 2 * SG_WIDTH)),
            _const_spec((1, 2 * SG_WIDTH)),
            _const_spec((1, SG_WIDTH)),
            _const_spec((1, SG_WIDTH)),
            _const_spec((SG_GROUPS, SG_CHUNK, SG_CHUNK)),
            _const_spec((SG_CHUNK, SG_WIDTH)),
            _const_spec((SG_WIDTH, D)),
            _const_spec((D, 2 * FFN_HIDDEN)),
            _const_spec((FFN_HIDDEN, D)),
        ],
        out_specs=pl.BlockSpec((tm, D), lambda i: (prev(i), 0)),
        out_shape=jax.ShapeDtypeStruct((B * L, D), F32),
        scratch_shapes=[pltpu.VMEM((tm, SG_WIDTH), BF16), pltpu.VMEM((2, tm, D), F32),
                        pltpu.VMEM((tm, FFN_HIDDEN), BF16)],
        compiler_params=pltpu.CompilerParams(
            dimension_semantics=("arbitrary",), vmem_limit_bytes=VMEM_LIMIT),
        name="sgu_layer",
    )(x.reshape(B * L, D), mod_l, mod_l, ng_l, w_in, b_in, ln_g, ln_b, w_s, bs_full, w_out,
      wab, wo)
    return out.reshape(B, L, D)


def _prep_ffn(ffn_w_in, ffn_w_out, i):
    return ffn_w_in[i].astype(BF16), ffn_w_out[i].astype(BF16)


def _prep_gate(gla_w_gk1, gla_w_gk2, j):
    wg1 = jnp.zeros((D, 2 * RANK_PAD), BF16)
    wg1 = wg1.at[:, :RANK].set(gla_w_gk1[j, 0].astype(BF16))
    wg1 = wg1.at[:, RANK_PAD:RANK_PAD + RANK].set(gla_w_gk1[j, 1].astype(BF16))
    wg2 = jnp.zeros((2, RANK_PAD, DK), BF16).at[:, :RANK, :].set(gla_w_gk2[j].astype(BF16))
    return wg1, wg2


def _trunk(x, mod, norm_g, gla_w_in, gla_w_gk1, gla_w_gk2, gla_b_gk, gla_g_head, gla_w_out,
           sg_w_in, sg_b_in, sg_ln_g, sg_ln_b, sg_w_s, sg_b_s, sg_w_out, ffn):
    B = x.shape[0]
    for i in range(DEPTH):
        mod_l = mod[i].reshape(B, 1, 6 * D)
        wab, wo = ffn[i]
        j = i // 2
        if i % 2 == 0:
            wg1, wg2 = _prep_gate(gla_w_gk1, gla_w_gk2, j)
            q, k, v, r, g1b, o_f = _gla_a_call(
                x, mod_l, norm_g[i], gla_w_in[j].astype(BF16), wg1, wg2, gla_b_gk[j])
            x = _gla_b_call(x, mod_l, norm_g[i], q, k, v, r, g1b, o_f, wg2, gla_b_gk[j],
                            gla_g_head[j].reshape(1, HV), gla_w_out[j].astype(BF16), wab, wo)
        else:
            bs_full = jnp.repeat(sg_b_s[j].T, SG_GD, axis=1)
            x = _sgu_call(x, mod_l, norm_g[i], sg_w_in[j].astype(BF16),
                          sg_b_in[j].reshape(1, -1), sg_ln_g[j].reshape(1, -1),
                          sg_ln_b[j].reshape(1, -1), sg_w_s[j].astype(BF16), bs_full,
                          sg_w_out[j].astype(BF16), wab, wo)
    return x


def kernel(x_prompt, x_sample, c_prompt, c_sample, norm_g, w_ada, b_ada, gla_w_in, gla_w_gk1,
           gla_w_gk2, gla_b_gk, gla_g_head, gla_w_out, sg_w_in, sg_b_in, sg_ln_g, sg_ln_b,
           sg_w_s, sg_b_s, sg_w_out, ffn_w_in, ffn_w_out):
    bp, bs = c_prompt.shape[0], c_sample.shape[0]
    rows = -(-(bp + bs) // 16) * 16
    c_all = jnp.concatenate([c_prompt, c_sample, jnp.zeros((rows - bp - bs, D), F32)], axis=0)
    mod = _ada_call(c_all, w_ada, b_ada)
    ffn = [_prep_ffn(ffn_w_in, ffn_w_out, i) for i in range(DEPTH)]
    args = (norm_g, gla_w_in, gla_w_gk1, gla_w_gk2, gla_b_gk, gla_g_head, gla_w_out,
            sg_w_in, sg_b_in, sg_ln_g, sg_ln_b, sg_w_s, sg_b_s, sg_w_out, ffn)
    y_prompt = _trunk(x_prompt, mod[:, :bp], *args)
    y_sample = _trunk(x_sample, mod[:, bp:bp + bs], *args)
    return (y_prompt, y_sample)
```

```python
import functools

import jax
import jax.numpy as jnp
from jax import lax
from jax.experimental import pallas as pl
from jax.experimental.pallas import tpu as pltpu

D = 1024
DEPTH = 4
N_HEADS = 4
DK = 512
DV = 1024
HK = DK // N_HEADS
HV = DV // N_HEADS
RANK = 16
RANK_PAD = 128
TAU = 16.0
CHUNK = 64
SG_CHUNK = 128
SG_GROUPS = 4
SG_WIDTH = 1024
SG_GD = SG_WIDTH // SG_GROUPS
FFN_HIDDEN = 2816
FFN_CHUNK = 256
EPS = 1e-6

GLA_A_TILE = 1024
GLA_TILE = 512
SGU_TILE = 512
SGU_SUBTILES = 2
VMEM_LIMIT = 56 * 1024 * 1024
GLA_B_FFN_SPLIT = 5

F32 = jnp.float32
BF16 = jnp.bfloat16


def _dot(a, b):
    return jnp.dot(a, b, preferred_element_type=F32)


def _dot_nt(a, b):
    return lax.dot_general(a, b, (((1,), (1,)), ((), ())), preferred_element_type=F32)


def _dot_tn(a, b):
    return lax.dot_general(a, b, (((0,), (0,)), ((), ())), preferred_element_type=F32)


def _sigmoid(x):
    return 1.0 / (1.0 + jnp.exp(-x))


def _rms(x, g):
    ms = jnp.mean(x * x, axis=-1, keepdims=True)
    return x * lax.rsqrt(ms + EPS) * g


def _log_sigmoid(x):
    return jnp.minimum(x, 0.0) - jnp.log1p(jnp.exp(-jnp.abs(x)))


def _split_bf16(x):
    hi = x.astype(BF16)
    lo = (x - hi.astype(F32)).astype(BF16)
    return hi, lo


def _const_spec(shape):
    zeros = (0,) * len(shape)
    return pl.BlockSpec(shape, lambda *_: zeros, pipeline_mode=pl.Buffered(1))


def _row_halves(n):
    return [slice(0, n // 2), slice(n // 2, n)]


def _interleave(*gens, pattern=""):
    live = list(gens)
    pattern = [int(ch) for ch in pattern if ch != " "]

    def advance(g):
        try:
            next(g)
        except StopIteration:
            live.remove(g)

    for idx in pattern:
        if gens[idx] in live:
            advance(gens[idx])
    while live:
        for g in list(live):
            advance(g)


def _ada_kernel(c_ref, w_ref, b_ref, o_ref):
    c = c_ref[...]
    s = (c * _sigmoid(c)).astype(BF16)
    o_ref[0] = _dot(s, w_ref[0].astype(BF16)) + b_ref[0]


def _ada_call(c_all, w_ada, b_ada):
    bp = c_all.shape[0]
    return pl.pallas_call(
        _ada_kernel,
        grid=(DEPTH, 6),
        in_specs=[
            pl.BlockSpec((bp, D), lambda i, j: (0, 0)),
            pl.BlockSpec((1, D, D), lambda i, j: (i, 0, j)),
            pl.BlockSpec((1, 1, D), lambda i, j: (i, 0, j)),
        ],
        out_specs=pl.BlockSpec((1, bp, D), lambda i, j: (i, 0, j)),
        out_shape=jax.ShapeDtypeStruct((DEPTH, bp, 6 * D), F32),
        name="ada_mod",
    )(c_all, w_ada, b_ada.reshape(DEPTH, 1, 6 * D))


def _adaln_steps(x_ref, g, sc, sh, out):
    gs = g * (1.0 + sc)
    for rs in _row_halves(x_ref.shape[0]):
        out.append((_rms(x_ref[rs, :], gs) + sh).astype(BF16))
        yield


def _residual_steps(x_ref, y, gain, g, dst_ref):
    gg = gain * g
    for rs in _row_halves(x_ref.shape[0]):
        dst_ref[rs, :] = x_ref[rs, :] + _rms(y[rs, :], gg)
        yield


def _ffn_steps(x1_ref, mod_ref, ng_ref, wab_ref, wo_ref, act_scr, y_ref, split_chunks=0):
    sh2 = mod_ref[0, :, 3 * D:4 * D]
    sc2 = mod_ref[0, :, 4 * D:5 * D]
    g2 = mod_ref[0, :, 5 * D:6 * D]
    hbs = []
    yield from _adaln_steps(x1_ref, ng_ref[2:3, :], sc2, sh2, hbs)
    hb = jnp.concatenate(hbs, axis=0)
    for c0 in range(0, FFN_HIDDEN, FFN_CHUNK):
        c1 = min(c0 + FFN_CHUNK, FFN_HIDDEN)
        a = _dot(hb, wab_ref[:, c0:c1])
        if c0 // FFN_CHUNK < split_chunks:
            yield
        b = _dot(hb, wab_ref[:, FFN_HIDDEN + c0:FFN_HIDDEN + c1])
        act_scr[:, c0:c1] = (a * _sigmoid(a) * b).astype(BF16)
        yield
    ys = []
    for j in range(2):
        ys.append(_dot(act_scr[...], wo_ref[:, j * 512:(j + 1) * 512]))
        yield
    yield from _residual_steps(x1_ref, jnp.concatenate(ys, axis=-1), g2, ng_ref[3:4, :], y_ref)


def _chunk_decay_columns(rows):
    pad = jnp.zeros((128 - len(rows), 128), F32)
    m = jnp.concatenate(rows + [pad], axis=0)
    return jnp.exp(m.T)


def _decay_qk(q, k, b):
    return (q * jnp.exp(b)).astype(BF16), (k * jnp.exp(-b)).astype(BF16)


def _decay_chunk_end(k, b, *, reverse):
    nc = k.shape[0] // CHUNK
    k_e, rows = [], []
    for c in range(nc):
        r0 = c * CHUNK if reverse else c * CHUNK + CHUNK - 1
        tot = b[r0:r0 + 1, :]
        rs = slice(c * CHUNK, (c + 1) * CHUNK)
        k_e.append((k[rs, :] * jnp.exp(tot - b[rs, :])).astype(BF16))
        rows += [tot[:, h * HK:(h + 1) * HK] for h in range(N_HEADS)]
    return jnp.concatenate(k_e, axis=0), _chunk_decay_columns(rows)


def _scan_steps(qd_ref, kd_ref, ke_ref, v_ref, dec_ref, s_ref, out_fn, *, reverse):
    nc = qd_ref.shape[0] // CHUNK
    ri = lax.broadcasted_iota(jnp.int32, (CHUNK, CHUNK), 0)
    ci = lax.broadcasted_iota(jnp.int32, (CHUNK, CHUNK), 1)
    mask = (ci > ri) if reverse else (ci <= ri)
    dec = dec_ref[...]

    def independent(c):
        rs = slice(c * CHUNK, (c + 1) * CHUNK)
        res = []
        for h in range(N_HEADS):
            ks = slice(h * HK, (h + 1) * HK)
            vv = v_ref[rs, h * HV:(h + 1) * HV]
            res.append((_dot_nt(qd_ref[rs, ks], kd_ref[rs, ks]), _dot_tn(ke_ref[rs, ks], vv)))
        return res

    def dependent(c, res):
        rs = slice(c * CHUNK, (c + 1) * CHUNK)
        for h in range(N_HEADS):
            sc, u = res[h]
            vv = v_ref[rs, h * HV:(h + 1) * HV]
            s_old = s_ref[h]
            lhs = jnp.concatenate([qd_ref[rs, h * HK:(h + 1) * HK],
                                   jnp.where(mask, sc, 0.0).astype(BF16)], axis=1)
            o = _dot(lhs, jnp.concatenate([s_old.astype(BF16), vv], axis=0))
            out_fn(c, h, o)
            j = c * N_HEADS + h
            s_ref[h] = s_old * dec[:, j:j + 1] + u

    order = list(range(nc - 1, -1, -1) if reverse else range(nc))
    prev = None
    for c in order:
        cur = (c, independent(c))
        if prev is not None:
            dependent(*prev)
        prev = cur
        yield
    dependent(*prev)
    yield


CUMSUM_BLOCK = 256


def _cumsum_matrix(reverse):
    r = jnp.arange(CUMSUM_BLOCK)[:, None]
    c = jnp.arange(CUMSUM_BLOCK)[None, :]
    tri = (c >= r) if reverse else (c <= r)
    return ((r // CHUNK == c // CHUNK) & tri).astype(BF16)


def _chunk_cumsum(t, hi, lo):
    blocks = []
    for r0 in range(0, hi.shape[0], CUMSUM_BLOCK):
        rs = slice(r0, r0 + CUMSUM_BLOCK)
        blocks.append(_dot(t, hi[rs, :]) + _dot(t, lo[rs, :]))
    return jnp.concatenate(blocks, axis=0)


def _gla_a_kernel(x_ref, mod_ref, ng_ref, t_ref, win_ref, wg1_ref, wg2_ref, bg_ref,
                  q_ref, k_ref, v_ref, r_ref, g1b_ref, of_ref,
                  s_ref, qd_scr, kd_scr, ke_scr, v_scr, dec_scr, *, per_seq):
    i = pl.program_id(0)
    wr = i % 2
    rd = (i + 1) % 2

    @pl.when(i == 0)
    def _():
        qd_scr[1] = jnp.zeros(qd_scr.shape[1:], BF16)
        kd_scr[1] = jnp.zeros(kd_scr.shape[1:], BF16)
        ke_scr[1] = jnp.zeros(ke_scr.shape[1:], BF16)
        v_scr[1] = jnp.zeros(v_scr.shape[1:], BF16)
        dec_scr[1] = jnp.zeros(dec_scr.shape[1:], F32)

    @pl.when((i == 0) | ((i - 1) % per_seq == 0))
    def _():
        s_ref[...] = jnp.zeros_like(s_ref)

    def proj():
        hbs = []
        yield from _adaln_steps(x_ref, ng_ref[0:1, :], mod_ref[0, :, D:2 * D], mod_ref[0, :, 0:D],
                                hbs)
        hb = jnp.concatenate(hbs, axis=0)
        g1 = _dot(hb, wg1_ref[...])
        g1b_ref[...] = g1[:, RANK_PAD:].astype(BF16)
        g1f = g1[:, :RANK_PAD].astype(BF16)
        yield
        for j in range(2):
            cs = slice(j * 512, (j + 1) * 512)
            vj = _dot(hb, win_ref[:, 2 * DK + j * 512:2 * DK + (j + 1) * 512]).astype(BF16)
            v_ref[:, cs] = vj
            v_scr[wr, :, cs] = vj
            if j == 0:
                gate = _dot(g1f, wg2_ref[0]) + bg_ref[0:1, :]
                la_hi, la_lo = _split_bf16(_log_sigmoid(gate) * (1.0 / TAU))
            yield
        b = _chunk_cumsum(t_ref[...], la_hi, la_lo)
        r0 = 2 * DK + DV
        r_ref[:, 0:512] = _dot(hb, win_ref[:, r0:r0 + 512]).astype(BF16)
        yield
        q = _dot(hb, win_ref[:, 0:DK]) * (HK ** -0.5)
        yield
        k = _dot(hb, win_ref[:, DK:2 * DK])
        r_ref[:, 512:1024] = _dot(hb, win_ref[:, r0 + 512:r0 + 1024]).astype(BF16)
        yield
        q_ref[...] = q.astype(BF16)
        k_ref[...] = k.astype(BF16)
        qd_scr[wr], kd_scr[wr] = _decay_qk(q, k, b)
        yield
        ke_scr[wr], dec_scr[wr] = _decay_chunk_end(k, b, reverse=False)

    def put(c, h, o):
        of_ref[c * CHUNK:(c + 1) * CHUNK, h * HV:(h + 1) * HV] = o.astype(BF16)

    scan = _scan_steps(qd_scr.at[rd], kd_scr.at[rd], ke_scr.at[rd], v_scr.at[rd], dec_scr.at[rd],
                       s_ref, put, reverse=False)
    _interleave(scan, proj())


def _gla_a_call(x, mod_l, ng_l, w_in, wg1, wg2, bg):
    B, L, _ = x.shape
    tm = GLA_A_TILE
    n = (B * L) // tm
    per_seq = L // tm
    cur = lambda i: jnp.minimum(i, n - 1)
    prev = lambda i: jnp.maximum(i - 1, 0)
    tok = lambda w, f: pl.BlockSpec((tm, w), lambda i: (f(i), 0))
    out_shapes = (
        jax.ShapeDtypeStruct((B * L, DK), BF16),
        jax.ShapeDtypeStruct((B * L, DK), BF16),
        jax.ShapeDtypeStruct((B * L, DV), BF16),
        jax.ShapeDtypeStruct((B * L, DV), BF16),
        jax.ShapeDtypeStruct((B * L, RANK_PAD), BF16),
        jax.ShapeDtypeStruct((B * L, DV), BF16),
    )
    return pl.pallas_call(
        functools.partial(_gla_a_kernel, per_seq=per_seq),
        grid=(n + 1,),
        in_specs=[
            tok(D, cur),
            pl.BlockSpec((1, 1, 6 * D), lambda i: (cur(i) // per_seq, 0, 0)),
            _const_spec((4, D)),
            _const_spec((CUMSUM_BLOCK, CUMSUM_BLOCK)),
            _const_spec((D, 2 * DK + 2 * DV)),
            _const_spec((D, 2 * RANK_PAD)),
            _const_spec((2, RANK_PAD, DK)),
            _const_spec((2, DK)),
        ],
        out_specs=(tok(DK, cur), tok(DK, cur), tok(DV, cur), tok(DV, cur), tok(RANK_PAD, cur),
                   tok(DV, prev)),
        out_shape=out_shapes,
        scratch_shapes=[
            pltpu.VMEM((N_HEADS, HK, HV), F32),
            pltpu.VMEM((2, tm, DK), BF16), pltpu.VMEM((2, tm, DK), BF16),
            pltpu.VMEM((2, tm, DK), BF16), pltpu.VMEM((2, tm, DV), BF16),
            pltpu.VMEM((2, 128, 128), F32),
        ],
        compiler_params=pltpu.CompilerParams(
            dimension_semantics=("arbitrary",), vmem_limit_bytes=VMEM_LIMIT),
        name="gla_a",
    )(x.reshape(B * L, D), mod_l, ng_l, _cumsum_matrix(False), w_in, wg1, wg2, bg)


def _gla_b_kernel(x_ref, modm_ref, modf_ref, ng_ref, t_ref, q_ref, k_ref, v_ref, r_ref, g1b_ref,
                  of_ref, wg2_ref, bg_ref, gh_ref, wout_ref, wab_ref, wo_ref,
                  y_ref, s_ref, o_scr, m_scr, qd_scr, kd_scr, ke_scr, dec_scr, x1_scr, act_scr,
                  *, per_seq):
    i = pl.program_id(0)

    @pl.when(i == 0)
    def _():
        x1_scr[1] = jnp.zeros(x1_scr.shape[1:], F32)

    @pl.when(i % per_seq == 0)
    def _():
        s_ref[...] = jnp.zeros_like(s_ref)

    def mixer():
        gate = _dot(g1b_ref[...], wg2_ref[1]) + bg_ref[1:2, :]
        la_hi, la_lo = _split_bf16(_log_sigmoid(gate) * (1.0 / TAU))
        yield
        b = _chunk_cumsum(t_ref[...], la_hi, la_lo)
        k = k_ref[...].astype(F32)
        qd_scr[...], kd_scr[...] = _decay_qk(q_ref[...].astype(F32), k, b)
        yield
        ke_scr[...], dec_scr[...] = _decay_chunk_end(k, b, reverse=True)
        yield

        def put(c, h, o):
            rs = slice(c * CHUNK, (c + 1) * CHUNK)
            vs = slice(h * HV, (h + 1) * HV)
            o_scr[rs, vs] = o + of_ref[rs, vs].astype(F32)

        yield from _scan_steps(qd_scr, kd_scr, ke_scr, v_ref, dec_scr, s_ref, put, reverse=True)
        gh = gh_ref[...]
        for h in range(N_HEADS):
            vs = slice(h * HV, (h + 1) * HV)
            rh = r_ref[:, vs].astype(F32)
            m_scr[:, vs] = (_rms(o_scr[:, vs], gh) * (rh * _sigmoid(rh))).astype(BF16)
            yield
        ys = []
        for j in range(2):
            ys.append(_dot(m_scr[...], wout_ref[:, j * 512:(j + 1) * 512]))
            yield
        yield from _residual_steps(x_ref, jnp.concatenate(ys, axis=-1), modm_ref[0, :, 2 * D:3 * D],
                                   ng_ref[1:2, :], x1_scr.at[i % 2])

    ffn = _ffn_steps(x1_scr.at[(i + 1) % 2], modf_ref, ng_ref, wab_ref, wo_ref, act_scr, y_ref,
                     split_chunks=GLA_B_FFN_SPLIT)
    _interleave(ffn, mixer(), pattern="00")


def _gla_b_call(x, mod_l, ng_l, q, k, v, r, g1b, o_f, wg2, bg, g_head, w_out, wab, wo):
    B, L, _ = x.shape
    tm = GLA_TILE
    n = (B * L) // tm
    per_seq = L // tm
    tile = lambda j: (j // per_seq) * per_seq + (per_seq - 1 - j % per_seq)
    cur = lambda i: jnp.minimum(i, n - 1)
    prev = lambda i: jnp.maximum(i - 1, 0)
    tok = lambda w: pl.BlockSpec((tm, w), lambda i: (tile(cur(i)), 0))
    out = pl.pallas_call(
        functools.partial(_gla_b_kernel, per_seq=per_seq),
        grid=(n + 1,),
        in_specs=[
            tok(D),
            pl.BlockSpec((1, 1, 6 * D), lambda i: (cur(i) // per_seq, 0, 0)),
            pl.BlockSpec((1, 1, 6 * D), lambda i: (prev(i) // per_seq, 0, 0)),
            _const_spec((4, D)),
            _const_spec((CUMSUM_BLOCK, CUMSUM_BLOCK)),
            tok(DK), tok(DK), tok(DV), tok(DV), tok(RANK_PAD), tok(DV),
            _const_spec((2, RANK_PAD, DK)),
            _const_spec((2, DK)),
            _const_spec((1, HV)),
            _const_spec((DV, D)),
            _const_spec((D, 2 * FFN_HIDDEN)),
            _const_spec((FFN_HIDDEN, D)),
        ],
        out_specs=pl.BlockSpec((tm, D), lambda i: (tile(prev(i)), 0)),
        out_shape=jax.ShapeDtypeStruct((B * L, D), F32),
        scratch_shapes=[
            pltpu.VMEM((N_HEADS, HK, HV), F32),
            pltpu.VMEM((tm, DV), F32),
            pltpu.VMEM((tm, DV), BF16),
            pltpu.VMEM((tm, DK), BF16), pltpu.VMEM((tm, DK), BF16), pltpu.VMEM((tm, DK), BF16),
            pltpu.VMEM((128, 128), F32),
            pltpu.VMEM((2, tm, D), F32),
            pltpu.VMEM((tm, FFN_HIDDEN), BF16),
        ],
        compiler_params=pltpu.CompilerParams(
            dimension_semantics=("arbitrary",), vmem_limit_bytes=VMEM_LIMIT),
        name="gla_b",
    )(x.reshape(B * L, D), mod_l, mod_l, ng_l, _cumsum_matrix(True), q, k, v, r, g1b, o_f,
      wg2, bg, g_head, w_out, wab, wo)
    return out.reshape(B, L, D)


GELU_C = 0.7978845608028654


def _gelu_tanh(x):
    return x * (0.5 + 0.5 * jnp.tanh(x * (GELU_C + (GELU_C * 0.044715) * (x * x))))


def _sgu_kernel(x_ref, modm_ref, modf_ref, ng_ref, win_ref, bin_ref, lng_ref, lnb_ref, ws_ref,
                bs_ref, wout_ref, wab_ref, wo_ref, y_ref, m_scr_all, x1_scr, act_scr):
    i = pl.program_id(0)
    tm = x_ref.shape[0]

    @pl.when(i == 0)
    def _():
        x1_scr[1] = jnp.zeros((tm, D), F32)

    def mixer(x_ref, m_scr, x1_dst):
        tm = x_ref.shape[0]
        hbs = []
        yield from _adaln_steps(x_ref, ng_ref[0:1, :], modm_ref[0, :, D:2 * D],
                                modm_ref[0, :, 0:D], hbs)
        hb = jnp.concatenate(hbs, axis=0)
        zs = []
        for j in range(4):
            cs = slice(j * 512, (j + 1) * 512)
            zs.append(_gelu_tanh(_dot(hb, win_ref[:, cs]) + bin_ref[:, cs]))
            yield
        vns = []
        for rs in _row_halves(tm):
            v = jnp.concatenate([zs[2][rs, :], zs[3][rs, :]], axis=-1)
            mu = jnp.mean(v, axis=-1, keepdims=True)
            vc = v - mu
            var = jnp.mean(vc * vc, axis=-1, keepdims=True)
            vns.append((vc * lax.rsqrt(var + EPS) * lng_ref[...] + lnb_ref[...]).astype(BF16))
            yield
        vn = jnp.concatenate(vns, axis=0)
        for n in range(tm // SG_CHUNK):
            rs = slice(n * SG_CHUNK, (n + 1) * SG_CHUNK)
            for g in range(SG_GROUPS):
                cs = slice(g * SG_GD, (g + 1) * SG_GD)
                u = zs[g // 2][rs, (g % 2) * SG_GD:(g % 2 + 1) * SG_GD]
                mixed = _dot(ws_ref[g], vn[rs, cs]) + bs_ref[:, cs]
                m_scr[rs, cs] = (u * mixed).astype(BF16)
            yield
        ys = []
        for j in range(2):
            ys.append(_dot(m_scr[...], wout_ref[:, j * 512:(j + 1) * 512]))
            yield
        yield from _residual_steps(x_ref, jnp.concatenate(ys, axis=-1), modm_ref[0, :, 2 * D:3 * D],
                                   ng_ref[1:2, :], x1_dst)

    sub = tm // SGU_SUBTILES
    for t in range(SGU_SUBTILES):
        rows = pl.ds(t * sub, sub)
        ffn = _ffn_steps(x1_scr.at[(i + 1) % 2].at[rows], modf_ref, ng_ref, wab_ref, wo_ref,
                         act_scr.at[rows], y_ref.at[rows])
        _interleave(ffn, mixer(x_ref.at[rows], m_scr_all.at[rows], x1_scr.at[i % 2].at[rows]))


def _sgu_call(x, mod_l, ng_l, w_in, b_in, ln_g, ln_b, w_s, bs_full, w_out, wab, wo):
    B, L, _ = x.shape
    tm = SGU_TILE
    n = (B * L) // tm
    per_seq = L // tm
    cur = lambda i: jnp.minimum(i, n - 1)
    prev = lambda i: jnp.maximum(i - 1, 0)
    out = pl.pallas_call(
        _sgu_kernel,
        grid=(n + 1,),
        in_specs=[
            pl.BlockSpec((tm, D), lambda i: (cur(i), 0)),
            pl.BlockSpec((1, 1, 6 * D), lambda i: (cur(i) // per_seq, 0, 0)),
            pl.BlockSpec((1, 1, 6 * D), lambda i: (prev(i) // per_seq, 0, 0)),
            _const_spec((4, D)),
            _const_spec((D, 2 * SG_WIDTH)),
            _const_spec((1, 2 * SG_WIDTH)),
            _const_spec((1, SG_WIDTH)),
            _const_spec((1, SG_WIDTH)),
            _const_spec((SG_GROUPS, SG_CHUNK, SG_CHUNK)),
            _const_spec((SG_CHUNK, SG_WIDTH)),
            _const_spec((SG_WIDTH, D)),
            _const_spec((D, 2 * FFN_HIDDEN)),
            _const_spec((FFN_HIDDEN, D)),
        ],
        out_specs=pl.BlockSpec((tm, D), lambda i: (prev(i), 0)),
        out_shape=jax.ShapeDtypeStruct((B * L, D), F32),
        scratch_shapes=[pltpu.VMEM((tm, SG_WIDTH), BF16), pltpu.VMEM((2, tm, D), F32),
                        pltpu.VMEM((tm, FFN_HIDDEN), BF16)],
        compiler_params=pltpu.CompilerParams(
            dimension_semantics=("arbitrary",), vmem_limit_bytes=VMEM_LIMIT),
        name="sgu_layer",
    )(x.reshape(B * L, D), mod_l, mod_l, ng_l, w_in, b_in, ln_g, ln_b, w_s, bs_full, w_out,
      wab, wo)
    return out.reshape(B, L, D)


def _prep_ffn(ffn_w_in, ffn_w_out, i):
    return ffn_w_in[i].astype(BF16), ffn_w_out[i].astype(BF16)


def _prep_gate(gla_w_gk1, gla_w_gk2, j):
    wg1 = jnp.zeros((D, 2 * RANK_PAD), BF16)
    wg1 = wg1.at[:, :RANK].set(gla_w_gk1[j, 0].astype(BF16))
    wg1 = wg1.at[:, RANK_PAD:RANK_PAD + RANK].set(gla_w_gk1[j, 1].astype(BF16))
    wg2 = jnp.zeros((2, RANK_PAD, DK), BF16).at[:, :RANK, :].set(gla_w_gk2[j].astype(BF16))
    return wg1, wg2


def _trunk(x, mod, norm_g, gla_w_in, gla_w_gk1, gla_w_gk2, gla_b_gk, gla_g_head, gla_w_out,
           sg_w_in, sg_b_in, sg_ln_g, sg_ln_b, sg_w_s, sg_b_s, sg_w_out, ffn):
    B = x.shape[0]
    for i in range(DEPTH):
        mod_l = mod[i].reshape(B, 1, 6 * D)
        wab, wo = ffn[i]
        j = i // 2
        if i % 2 == 0:
            wg1, wg2 = _prep_gate(gla_w_gk1, gla_w_gk2, j)
            q, k, v, r, g1b, o_f = _gla_a_call(
                x, mod_l, norm_g[i], gla_w_in[j].astype(BF16), wg1, wg2, gla_b_gk[j])
            x = _gla_b_call(x, mod_l, norm_g[i], q, k, v, r, g1b, o_f, wg2, gla_b_gk[j],
                            gla_g_head[j].reshape(1, HV), gla_w_out[j].astype(BF16), wab, wo)
        else:
            bs_full = jnp.repeat(sg_b_s[j].T, SG_GD, axis=1)
            x = _sgu_call(x, mod_l, norm_g[i], sg_w_in[j].astype(BF16),
                          sg_b_in[j].reshape(1, -1), sg_ln_g[j].reshape(1, -1),
                          sg_ln_b[j].reshape(1, -1), sg_w_s[j].astype(BF16), bs_full,
                          sg_w_out[j].astype(BF16), wab, wo)
    return x


def kernel(x_prompt, x_sample, c_prompt, c_sample, norm_g, w_ada, b_ada, gla_w_in, gla_w_gk1,
           gla_w_gk2, gla_b_gk, gla_g_head, gla_w_out, sg_w_in, sg_b_in, sg_ln_g, sg_ln_b,
           sg_w_s, sg_b_s, sg_w_out, ffn_w_in, ffn_w_out):
    bp, bs = c_prompt.shape[0], c_sample.shape[0]
    rows = -(-(bp + bs) // 16) * 16
    c_all = jnp.concatenate([c_prompt, c_sample, jnp.zeros((rows - bp - bs, D), F32)], axis=0)
    mod = _ada_call(c_all, w_ada, b_ada)
    ffn = [_prep_ffn(ffn_w_in, ffn_w_out, i) for i in range(DEPTH)]
    args = (norm_g, gla_w_in, gla_w_gk1, gla_w_gk2, gla_b_gk, gla_g_head, gla_w_out,
            sg_w_in, sg_b_in, sg_ln_g, sg_ln_b, sg_w_s, sg_b_s, sg_w_out, ffn)
    y_prompt = _trunk(x_prompt, mod[:, :bp], *args)
    y_sample = _trunk(x_sample, mod[:, bp:bp + bs], *args)
    return (y_prompt, y_sample)
```
